```python
import math
import jax, jax.numpy as jnp
from jax import lax
import numpy as np

D_MODEL = 1024
BATCH = 4
SEQ = 4096
DEPTH = 4
DEC_BATCH = 128
DEC_SEQ = 4
PAST_LEN = 2048
PAGE_SIZE = 128

H_A = 8
DK_A = 64
DV_A = 64
W_QK_A = H_A * DK_A
W_V_A = H_A * DV_A
CONV_W = 4
CONV_CH = 2 * W_QK_A + W_V_A
DELTA_CHUNK = 64
H_B = 8
D_HB = 64
W_B = H_B * D_HB
SB_BLOCK = 128
SB_BIAS_INIT = -6.0
D_FF = 2816
N_EXPERTS = 8
TOP_K = 2
D_FF_EXPERT = 3584
N_DENSE = (DEPTH + 1) // 2
N_MOE = DEPTH // 2
DN_ALPHA = (2.0 * DEPTH) ** 0.25
DN_BETA = (8.0 * DEPTH) ** -0.25
LN_EPS = 1e-5
NORM_EPS = 1e-6
SPLIT_WIDTHS = (CONV_CH, W_V_A, H_A, H_A, W_B, W_B, W_B, D_MODEL, D_MODEL)
P_IN = sum(SPLIT_WIDTHS)

kernel_name = "gated_delta_stickbreak_hybrid_step"


def layer_norm(x, g, b):
    xf = x.astype(jnp.float32)
    mu = jnp.mean(xf, axis=-1, keepdims=True)
    xc = xf - mu
    var = jnp.mean(xc * xc, axis=-1, keepdims=True)
    return (xc * lax.rsqrt(var + LN_EPS) * g + b).astype(x.dtype)


def l2_normalize(x):
    xf = x.astype(jnp.float32)
    return xf * lax.rsqrt(jnp.sum(xf * xf, axis=-1, keepdims=True) + NORM_EPS)


def gated_rms_norm(o, w, z):
    of = o.astype(jnp.float32)
    r = of * lax.rsqrt(jnp.mean(of * of, axis=-1, keepdims=True) + NORM_EPS)
    return r * w * jax.nn.silu(z.astype(jnp.float32))


def causal_conv_silu(x, buf, w):
    L = x.shape[1]
    xp = jnp.concatenate([buf.astype(x.dtype), x], axis=1)
    y = xp[:, 0:L] * w[0]
    for i in range(1, CONV_W):
        y = y + xp[:, i:i + L] * w[i]
    return jax.nn.silu(y), xp[:, L:]


def gated_delta_chunked(q, k, v, beta, g, s0):
    bsz, L, H, _ = q.shape
    dv = v.shape[-1]
    C = min(DELTA_CHUNK, L)
    n = -(-L // C)
    pad = n * C - L
    if pad:
        q, k, v, beta, g = [jnp.pad(a, [(0, 0), (0, pad)] + [(0, 0)] * (a.ndim - 2)) for a in (q, k, v, beta, g)]
    to_chunks = lambda a: jnp.transpose(a.reshape(bsz, n, C, H, a.shape[-1]), (1, 0, 3, 2, 4))
    qc, kc, vc = to_chunks(q), to_chunks(k), to_chunks(v)
    bc = jnp.transpose(beta.reshape(bsz, n, C, H), (1, 0, 3, 2))
    gc = jnp.cumsum(jnp.transpose(g.reshape(bsz, n, C, H), (1, 0, 3, 2)), axis=-1)
    incl = jnp.tril(jnp.ones((C, C), dtype=bool))
    strict = jnp.tril(jnp.ones((C, C), dtype=bool), -1)
    decay = jnp.exp(jnp.where(incl, gc[..., :, None] - gc[..., None, :], -jnp.inf))
    kb = kc * bc[..., None]
    m = jnp.where(strict, jnp.einsum('nbhid,nbhjd->nbhij', kb, kc) * decay, 0.0)
    rhs = jnp.concatenate([vc * bc[..., None], kb * jnp.exp(gc)[..., None]], axis=-1)
    sol = lax.linalg.triangular_solve(m + jnp.eye(C, dtype=m.dtype), rhs,
                                      left_side=True, lower=True, unit_diagonal=True)
    u, w = sol[..., :dv], sol[..., dv:]
    qk = jnp.where(incl, jnp.einsum('nbhid,nbhjd->nbhij', qc, kc) * decay, 0.0)

    def step(S, inp):
        q_i, k_i, u_i, w_i, g_i, qk_i = inp
        v_new = u_i - jnp.einsum('bhck,bhkv->bhcv', w_i, S)
        o_i = (jnp.einsum('bhck,bhkv->bhcv', q_i * jnp.exp(g_i)[..., None], S)
               + jnp.einsum('bhij,bhjv->bhiv', qk_i, v_new))
        g_last = g_i[..., -1]
        k_dec = k_i * jnp.exp(g_last[..., None] - g_i)[..., None]
        S = S * jnp.exp(g_last)[..., None, None] + jnp.einsum('bhck,bhcv->bhkv', k_dec, v_new)
        return S, o_i

    s_fin, o = lax.scan(step, s0, (qc, kc, u, w, gc, qk))
    o = jnp.transpose(o, (1, 0, 3, 2, 4)).reshape(bsz, n * C, H, dv)[:, :L]
    return o, s_fin


def stick_breaking_attention(q, k, v, q_pos, k_pos, bias):
    bsz, Lq, H, dh = q.shape
    qb_len = min(SB_BLOCK, Lq)
    nb = -(-Lq // qb_len)
    pad = nb * qb_len - Lq
    if pad:
        q = jnp.pad(q, ((0, 0), (0, pad), (0, 0), (0, 0)))
        q_pos = jnp.pad(q_pos, (0, pad), mode='edge')
    q_blocks = jnp.moveaxis(q.reshape(bsz, nb, qb_len, H, dh), 1, 0)
    pos_blocks = q_pos.reshape(nb, qb_len)
    scale = dh ** -0.5
    bias_f = bias.astype(jnp.float32)[None, :, None, None]

    def one_block(args):
        qi, pi = args
        z = jnp.einsum('bqhd,bkhd->bhqk', qi, k).astype(jnp.float32) * scale + bias_f
        mask = k_pos[None, :] < pi[:, None]
        log_keep = jnp.where(mask, jax.nn.log_sigmoid(-z), 0.0)
        log_after = lax.cumsum(log_keep, axis=3, reverse=True) - log_keep
        a = jnp.where(mask, jnp.exp(jax.nn.log_sigmoid(z) + log_after), 0.0)
        return jnp.einsum('bhqk,bkhd->bqhd', a.astype(v.dtype), v)

    o = lax.map(one_block, (q_blocks, pos_blocks))
    o = jnp.moveaxis(o, 0, 1).reshape(bsz, nb * qb_len, H, dh)[:, :Lq]
    return o.astype(q.dtype)


def token_mixer(u, conv_buf, s0, past_k, past_v, p):
    bsz, L, _ = u.shape
    proj = u @ p['w_in']
    idx = np.cumsum(SPLIT_WIDTHS)[:-1].tolist()
    qkv_a, z_a, b_a, a_a, q_b, k_b, v_b, gl_a, gl_b = jnp.split(proj, idx, axis=-1)
    qkv_c, new_buf = causal_conv_silu(qkv_a, conv_buf, p['conv_w'])
    q_a, k_a, v_a = jnp.split(qkv_c, [W_QK_A, 2 * W_QK_A], axis=-1)
    q_a = l2_normalize(q_a.reshape(bsz, L, H_A, DK_A)) * (DK_A ** -0.5)
    k_a = l2_normalize(k_a.reshape(bsz, L, H_A, DK_A))
    v_a = v_a.reshape(bsz, L, H_A, DV_A).astype(jnp.float32)
    beta = jax.nn.sigmoid(b_a.astype(jnp.float32))
    g = -jnp.exp(p['a_log'].astype(jnp.float32)) * jax.nn.softplus(a_a.astype(jnp.float32) + p['dt_bias'])
    o_a, s_new = gated_delta_chunked(q_a, k_a, v_a, beta, g, s0.astype(jnp.float32))
    o_a = gated_rms_norm(o_a, p['o_norm_w'], z_a.reshape(bsz, L, H_A, DV_A))
    o_a = o_a.reshape(bsz, L, W_V_A).astype(u.dtype)
    q_b = q_b.reshape(bsz, L, H_B, D_HB)
    k_b = k_b.reshape(bsz, L, H_B, D_HB)
    v_b = v_b.reshape(bsz, L, H_B, D_HB)
    if past_k is None:
        past_len = 0
        keys, vals = k_b, v_b
    else:
        past_len = past_k.shape[1]
        keys = jnp.concatenate([past_k.astype(u.dtype), k_b], axis=1)
        vals = jnp.concatenate([past_v.astype(u.dtype), v_b], axis=1)
    q_pos = past_len + jnp.arange(L, dtype=jnp.int32)
    k_pos = jnp.arange(keys.shape[1], dtype=jnp.int32)
    o_b = stick_breaking_attention(q_b, keys, vals, q_pos, k_pos, p['sb_bias']).reshape(bsz, L, W_B)
    merged = jax.nn.sigmoid(gl_a) * (o_a @ p['w_proj_a']) + jax.nn.sigmoid(gl_b) * (o_b @ p['w_proj_b'])
    return merged @ p['w_out'], k_b, v_b, s_new.astype(u.dtype), new_buf


def swiglu(x, w_gate, w_up, w_down):
    return (jax.nn.silu(x @ w_gate) * (x @ w_up)) @ w_down


def moe_swiglu(x, router_w, w_gate, w_up, w_down):
    logits = (x @ router_w).astype(jnp.float32)
    top_logit, top_idx = lax.top_k(logits, TOP_K)
    top_w = jax.nn.softmax(top_logit, axis=-1)
    gate = jnp.sum(jax.nn.one_hot(top_idx, N_EXPERTS, dtype=jnp.float32) * top_w[..., None], axis=-2)
    y = jnp.zeros(x.shape, jnp.float32)
    for e in range(N_EXPERTS):
        y = y + gate[..., e:e + 1] * swiglu(x, w_gate[e], w_up[e], w_down[e])
    return y.astype(x.dtype)


def trunk_layer(x, conv_buf, s0, past_k, past_v, p, channel_mixer, ln):
    y, k_new, v_new, s_new, buf_new = token_mixer(x, conv_buf, s0, past_k, past_v, p)
    x = layer_norm(DN_ALPHA * x + y, ln[0], ln[1])
    x = layer_norm(DN_ALPHA * x + channel_mixer(x), ln[2], ln[3])
    return x, k_new, v_new, s_new, buf_new


def setup_inputs(seed: int = 0) -> dict:
    key = jax.random.key(seed)
    ks = iter(jax.random.split(key, 32))
    f32 = jnp.float32
    nrm = lambda shape, scale: jax.random.normal(next(ks), shape, f32) * scale
    n_pages = PAST_LEN // PAGE_SIZE
    n_used = DEC_BATCH * n_pages
    n_phys = n_used + n_used // 4
    x_prompt = nrm((BATCH, SEQ, D_MODEL), 1.0)
    x_sample = nrm((DEC_BATCH, DEC_SEQ, D_MODEL), 1.0)
    cache_k = nrm((DEPTH, n_phys, PAGE_SIZE, H_B, D_HB), 1.0)
    cache_v = nrm((DEPTH, n_phys, PAGE_SIZE, H_B, D_HB), 1.0)
    state_delta = nrm((DEPTH, DEC_BATCH, H_A, DK_A, DV_A), 0.3)
    state_conv = nrm((DEPTH, DEC_BATCH, CONV_W - 1, CONV_CH), 1.0)
    page_table = jax.random.permutation(next(ks), n_phys)[:n_used].reshape(DEC_BATCH, n_pages).astype(jnp.int32)
    w_in = nrm((DEPTH, D_MODEL, P_IN), D_MODEL ** -0.5)
    conv_w = nrm((DEPTH, CONV_W, CONV_CH), CONV_W ** -0.5)
    a_log = jnp.log(jax.random.uniform(next(ks), (DEPTH, H_A), f32, 1.0, 16.0))
    dt = jnp.exp(jax.random.uniform(next(ks), (DEPTH, H_A), f32, math.log(1e-3), math.log(1e-1)))
    dt_bias = dt + jnp.log(-jnp.expm1(-dt))
    o_norm_w = 1.0 + nrm((DEPTH, DV_A), 0.02)
    sb_bias = SB_BIAS_INIT + nrm((DEPTH, H_B), 0.5)
    w_proj_a = nrm((DEPTH, W_V_A, D_MODEL), W_V_A ** -0.5 * DN_BETA)
    w_proj_b = nrm((DEPTH, W_B, D_MODEL), W_B ** -0.5 * DN_BETA)
    w_out = nrm((DEPTH, D_MODEL, D_MODEL), D_MODEL ** -0.5 * DN_BETA)
    ln1_g = 1.0 + nrm((DEPTH, D_MODEL), 0.02)
    ln1_b = nrm((DEPTH, D_MODEL), 0.02)
    ln2_g = 1.0 + nrm((DEPTH, D_MODEL), 0.02)
    ln2_b = nrm((DEPTH, D_MODEL), 0.02)
    ffn_w_gate = nrm((N_DENSE, D_MODEL, D_FF), D_MODEL ** -0.5)
    ffn_w_up = nrm((N_DENSE, D_MODEL, D_FF), D_MODEL ** -0.5)
    ffn_w_down = nrm((N_DENSE, D_FF, D_MODEL), D_FF ** -0.5 * DN_BETA)
    router_w = nrm((N_MOE, D_MODEL, N_EXPERTS), D_MODEL ** -0.5)
    moe_w_gate = nrm((N_MOE, N_EXPERTS, D_MODEL, D_FF_EXPERT), D_MODEL ** -0.5)
    moe_w_up = nrm((N_MOE, N_EXPERTS, D_MODEL, D_FF_EXPERT), D_MODEL ** -0.5)
    moe_w_down = nrm((N_MOE, N_EXPERTS, D_FF_EXPERT, D_MODEL), D_FF_EXPERT ** -0.5 * DN_BETA)
    return {"x_prompt": x_prompt, "x_sample": x_sample, "cache_k": cache_k, "cache_v": cache_v,
            "state_delta": state_delta, "state_conv": state_conv, "page_table": page_table,
            "w_in": w_in, "conv_w": conv_w, "a_log": a_log, "dt_bias": dt_bias, "o_norm_w": o_norm_w,
            "sb_bias": sb_bias, "w_proj_a": w_proj_a, "w_proj_b": w_proj_b, "w_out": w_out,
            "ln1_g": ln1_g, "ln1_b": ln1_b, "ln2_g": ln2_g, "ln2_b": ln2_b,
            "ffn_w_gate": ffn_w_gate, "ffn_w_up": ffn_w_up, "ffn_w_down": ffn_w_down,
            "router_w": router_w, "moe_w_gate": moe_w_gate, "moe_w_up": moe_w_up, "moe_w_down": moe_w_down}


def reference(x_prompt, x_sample, cache_k, cache_v, state_delta, state_conv, page_table,
              w_in, conv_w, a_log, dt_bias, o_norm_w, sb_bias, w_proj_a, w_proj_b, w_out,
              ln1_g, ln1_b, ln2_g, ln2_b, ffn_w_gate, ffn_w_up, ffn_w_down,
              router_w, moe_w_gate, moe_w_up, moe_w_down):
    n_pages = page_table.shape[1]
    past_len = n_pages * PAGE_SIZE
    pb, db = x_prompt.shape[0], x_sample.shape[0]
    xp, xs = x_prompt, x_sample
    buf0 = jnp.zeros((pb, CONV_W - 1, CONV_CH), x_prompt.dtype)
    s00 = jnp.zeros((pb, H_A, DK_A, DV_A), jnp.float32)
    kp_l, vp_l, sp_l, cp_l = [], [], [], []
    ks_l, vs_l, ss_l, cs_l = [], [], [], []
    for li in range(DEPTH):
        p = {'w_in': w_in[li], 'conv_w': conv_w[li], 'a_log': a_log[li], 'dt_bias': dt_bias[li],
             'o_norm_w': o_norm_w[li], 'sb_bias': sb_bias[li], 'w_proj_a': w_proj_a[li],
             'w_proj_b': w_proj_b[li], 'w_out': w_out[li]}
        j = li // 2
        if li % 2 == 0:
            ffn = lambda x, j=j: swiglu(x, ffn_w_gate[j], ffn_w_up[j], ffn_w_down[j])
        else:
            ffn = lambda x, j=j: moe_swiglu(x, router_w[j], moe_w_gate[j], moe_w_up[j], moe_w_down[j])
        ln = (ln1_g[li], ln1_b[li], ln2_g[li], ln2_b[li])
        xp, kp, vp, sp, cp = trunk_layer(xp, buf0, s00, None, None, p, ffn, ln)
        pk = cache_k[li][page_table].reshape(db, past_len, H_B, D_HB)
        pv = cache_v[li][page_table].reshape(db, past_len, H_B, D_HB)
        xs, kn, vn, sn, cn = trunk_layer(xs, state_conv[li], state_delta[li], pk, pv, p, ffn, ln)
        kp_l.append(kp); vp_l.append(vp); sp_l.append(sp); cp_l.append(cp)
        ks_l.append(kn); vs_l.append(vn); ss_l.append(sn); cs_l.append(cn)
    k_prompt = jnp.stack(kp_l)
    v_prompt = jnp.stack(vp_l)
    delta_prompt = jnp.stack(sp_l)
    conv_prompt = jnp.stack(cp_l)
    k_sample = jnp.stack(ks_l)
    v_sample = jnp.stack(vs_l)
    delta_sample = jnp.stack(ss_l)
    conv_sample = jnp.stack(cs_l)
    return (xp, xs, k_prompt, v_prompt, delta_prompt, conv_prompt, k_sample, v_sample, delta_sample, conv_sample)
```

```python
import functools
import math

import jax
import jax.numpy as jnp
from jax import lax
from jax.experimental import pallas as pl
from jax.experimental.pallas import tpu as pltpu

F32 = jnp.float32
BF16 = jnp.bfloat16

H_A = 8
DK_A = 64
DV_A = 64
H_B = 8
D_HB = 64
CONV_W = 4
DELTA_CHUNK = 64
N_EXPERTS = 8
LN_EPS = 1e-5
NORM_EPS = 1e-6
SAMPLE_PAD = 8
LANE = 128
VMEM_LIMIT = 56 * 1024 * 1024


def _bf(x):
    return x.astype(BF16)


def _dot(a, b):
    return jnp.dot(a, b, preferred_element_type=F32)


def _dot_nt(a, b):
    return lax.dot_general(a, b, (((1,), (1,)), ((), ())), preferred_element_type=F32)


def _dot_tn(a, b):
    return lax.dot_general(a, b, (((0,), (0,)), ((), ())), preferred_element_type=F32)


def _split2(x):
    hi = _bf(x)
    lo = _bf(x - hi.astype(F32))
    return hi, lo


def _split3(x):
    hi = _bf(x)
    r = x - hi.astype(F32)
    mid = _bf(r)
    lo = _bf(r - mid.astype(F32))
    return hi, mid, lo


def _dot3(a, b):
    ah, al = _split2(a)
    bh, bl = _split2(b)
    return _dot(ah, bh) + (_dot(ah, bl) + _dot(al, bh))


def _silu(x):
    return x * (1.0 / (1.0 + jnp.exp(-x)))


def _sigmoid(x):
    return 1.0 / (1.0 + jnp.exp(-x))


def _softplus(x):
    return jnp.maximum(x, 0.0) + jnp.log1p(jnp.exp(-jnp.abs(x)))


def _layer_norm(x, g, b):
    mu = jnp.mean(x, axis=-1, keepdims=True)
    xc = x - mu
    var = jnp.mean(xc * xc, axis=-1, keepdims=True)
    return xc * lax.rsqrt(var + LN_EPS) * g + b


def _cparams(sem, vmem=VMEM_LIMIT):
    return pltpu.CompilerParams(dimension_semantics=sem, vmem_limit_bytes=vmem)


_PROJ_GROUPS = (("qkv", 1536), ("z", 512), ("gla", 1024), ("glb", 1024),
                ("qb", 512), ("kb", 512), ("vb", 512), ("bd", 128))


def _proj_kernel(x_ref, w_ref, *o_refs):
    x = _bf(x_ref[...])
    off = 0
    for (_, width), o_ref in zip(_PROJ_GROUPS, o_refs):
        o_ref[...] = _dot(x, w_ref[:, off:off + width])
        off += width


def in_projection(x, w_packed, tm=256):
    T, D = x.shape
    P = w_packed.shape[1]
    outs = [jax.ShapeDtypeStruct((T, w), F32) for _, w in _PROJ_GROUPS]
    return pl.pallas_call(
        _proj_kernel,
        grid=(T // tm,),
        in_specs=[pl.BlockSpec((tm, D), lambda i: (i, 0)),
                  pl.BlockSpec((D, P), lambda i: (0, 0))],
        out_specs=[pl.BlockSpec((tm, w), lambda i: (i, 0)) for _, w in _PROJ_GROUPS],
        out_shape=outs,
        compiler_params=_cparams(("parallel",)),
        name="in_projection",
    )(x, w_packed)


def pack_w_in(w_in):
    c = 3 * H_A * DK_A
    wv = H_A * DV_A
    wb = H_B * D_HB
    D = w_in.shape[0]
    i = 0
    qkv = w_in[:, i:i + c]; i += c
    z = w_in[:, i:i + wv]; i += wv
    bd = w_in[:, i:i + 2 * H_A]; i += 2 * H_A
    qb = w_in[:, i:i + wb]; i += wb
    kb = w_in[:, i:i + wb]; i += wb
    vb = w_in[:, i:i + wb]; i += wb
    gla = w_in[:, i:i + D]; i += D
    glb = w_in[:, i:i + D]; i += D
    bd = jnp.pad(bd, ((0, 0), (0, LANE - 2 * H_A)))
    return _bf(jnp.concatenate([qkv, z, gla, glb, qb, kb, vb, bd], axis=1))


def _prep_kernel(x_ref, w_ref, buf_ref, hsum_ref, q_ref, k_ref, v_ref, nb_ref, xp_ref,
                 *, tt, valid_last):
    t = pl.program_id(1)
    nt = pl.num_programs(1)
    halo = CONV_W - 1
    base = 8

    @pl.when(t == 0)
    def _():
        xp_ref[base - halo:base, :] = buf_ref[0]

    @pl.when(t > 0)
    def _():
        xp_ref[base - halo:base, :] = xp_ref[base + tt - halo:base + tt, :]

    xp_ref[base:base + tt, :] = x_ref[...]
    w = w_ref[...]
    y = xp_ref[base - halo:base - halo + tt, :] * w[0:1, :]
    for i in range(1, CONV_W):
        y = y + xp_ref[base - halo + i:base - halo + i + tt, :] * w[i:i + 1, :]
    y = _silu(y)
    wq = H_A * DK_A
    q = y[:, :wq]
    k = y[:, wq:2 * wq]
    v_ref[...] = y[:, 2 * wq:]
    hs = hsum_ref[...]

    def l2n(a):
        hi, lo = _split2(a * a)
        ss = _dot(hi, hs) + _dot(lo, hs)
        return a * lax.rsqrt(ss + NORM_EPS)

    q_ref[...] = l2n(q) * (DK_A ** -0.5)
    k_ref[...] = l2n(k)

    @pl.when(t == nt - 1)
    def _():
        nb_ref[0] = xp_ref[base + valid_last - halo:base + valid_last, :]


def conv_prep(qkv, conv_w, buf, row0, nseq, seqlen, tt, valid_len):
    T, C = qkv.shape
    W = C // 3
    nt = seqlen // tt
    rb0 = row0 // tt
    valid_last = valid_len - (nt - 1) * tt
    head = jnp.arange(W, dtype=jnp.int32) // DK_A
    hsum = (head[:, None] == head[None, :]).astype(BF16)
    row_map = lambda b, t: (rb0 + b * nt + t, 0)
    out_rows = nseq * seqlen
    out_map = lambda b, t: (b * nt + t, 0)
    kern = functools.partial(_prep_kernel, tt=tt, valid_last=valid_last)
    return pl.pallas_call(
        kern,
        grid=(nseq, nt),
        in_specs=[pl.BlockSpec((tt, C), row_map),
                  pl.BlockSpec((CONV_W, C), lambda b, t: (0, 0)),
                  pl.BlockSpec((1, CONV_W - 1, C), lambda b, t: (b, 0, 0)),
                  pl.BlockSpec((W, W), lambda b, t: (0, 0))],
        out_specs=[pl.BlockSpec((tt, W), out_map),
                   pl.BlockSpec((tt, W), out_map),
                   pl.BlockSpec((tt, W), out_map),
                   pl.BlockSpec((1, CONV_W - 1, C), lambda b, t: (b, 0, 0))],
        out_shape=[jax.ShapeDtypeStruct((out_rows, W), F32)] * 3
                  + [jax.ShapeDtypeStruct((nseq, CONV_W - 1, C), F32)],
        scratch_shapes=[pltpu.VMEM((tt + 8, C), F32)],
        compiler_params=_cparams(("parallel", "arbitrary")),
        name="conv_prep",
    )(qkv, conv_w, buf, hsum)


def _delta_kernel(q_ref, k_ref, v_ref, bd_ref, z_ref, hp_ref, nw_ref, s0_ref,
                  o_ref, sf_ref, s_scr, *, C, valid_len, seqlen, nstage):
    c = pl.program_id(1)
    nc = pl.num_programs(1)

    @pl.when(c == 0)
    def _():
        s_scr[...] = s0_ref[0]

    row = lax.broadcasted_iota(jnp.int32, (C, C), 0)
    col = lax.broadcasted_iota(jnp.int32, (C, C), 1)
    incl = row >= col
    strict = row > col
    eye = row == col
    tri_incl = jnp.where(incl, 1.0, 0.0).astype(BF16)
    ones_cc = jnp.ones((C, C), BF16)

    bd = bd_ref[...]
    hp = hp_ref[...]
    beta_all = _sigmoid(bd)
    g_all = hp[0:1, :] * _softplus(bd + hp[1:2, :])
    if valid_len < seqlen:
        pos = c * C + lax.broadcasted_iota(jnp.int32, (C, LANE), 0)
        live = pos < valid_len
        beta_all = jnp.where(live, beta_all, 0.0)
        g_all = jnp.where(live, g_all, 0.0)
    gh, gm, gl = _split3(g_all)
    gc_all = _dot(tri_incl, gh) + (_dot(tri_incl, gm) + _dot(tri_incl, gl))

    nw = nw_ref[...]
    outs = []
    for h in range(H_A):
        sl = slice(h * DK_A, (h + 1) * DK_A)
        q = q_ref[:, sl]
        k = k_ref[:, sl]
        v = v_ref[:, sl]
        z = z_ref[:, sl]
        beta = beta_all[:, h:h + 1]
        gc = gc_all[:, H_A + h:H_A + h + 1]
        a_mat = jnp.broadcast_to(gc, (C, C))
        dh, dm, dl = _split3(jnp.where(eye, a_mat, 0.0))
        b_mat = _dot_nt(ones_cc, dh) + (_dot_nt(ones_cc, dm) + _dot_nt(ones_cc, dl))
        decay = jnp.where(incl, jnp.exp(jnp.minimum(a_mat - b_mat, 0.0)), 0.0)
        kb = k * beta
        m = jnp.where(strict, _dot_nt(_bf(kb), _bf(k)) * decay, 0.0)
        eg = jnp.exp(gc)
        r = jnp.concatenate([v * beta, kb * eg], axis=-1)
        p = -m
        r = r + _dot3(p, r)
        for _ in range(nstage - 1):
            p = _dot3(p, p)
            r = r + _dot3(p, r)
        u = r[:, :DV_A]
        w = r[:, DV_A:]
        qk = jnp.where(incl, _dot_nt(_bf(q), _bf(k)) * decay, 0.0)
        s = s_scr[h]
        sb = _bf(s)
        v_new = u - _dot(_bf(w), sb)
        o = _dot(_bf(q * eg), sb) + _dot(_bf(qk), _bf(v_new))
        g_last = gc[C - 1:C, :]
        k_dec = k * jnp.exp(g_last - gc)
        s_scr[h] = s * jnp.exp(g_last) + _dot_tn(_bf(k_dec), _bf(v_new))
        rn = o * lax.rsqrt(jnp.mean(o * o, axis=-1, keepdims=True) + NORM_EPS)
        outs.append(rn * nw * _silu(z))
    o_ref[...] = jnp.concatenate(outs, axis=-1)

    @pl.when(c == nc - 1)
    def _():
        sf_ref[0] = s_scr[...]


def delta_rule(q, k, v, bd, z, head_par, norm_w, s0, bd_row0, nseq, seqlen, C, valid_len):
    W = q.shape[1]
    nc = seqlen // C
    nstage = max(1, int(math.log2(C)))
    rb0 = bd_row0 // C
    loc = lambda b, c: (b * nc + c, 0)
    glob = lambda b, c: (rb0 + b * nc + c, 0)
    kern = functools.partial(_delta_kernel, C=C, valid_len=valid_len, seqlen=seqlen, nstage=nstage)
    return pl.pallas_call(
        kern,
        grid=(nseq, nc),
        in_specs=[pl.BlockSpec((C, W), loc), pl.BlockSpec((C, W), loc), pl.BlockSpec((C, W), loc),
                  pl.BlockSpec((C, LANE), glob), pl.BlockSpec((C, W), glob),
                  pl.BlockSpec((2, LANE), lambda b, c: (0, 0)),
                  pl.BlockSpec((1, DV_A), lambda b, c: (0, 0)),
                  pl.BlockSpec((1, H_A, DK_A, DV_A), lambda b, c: (b, 0, 0, 0))],
        out_specs=[pl.BlockSpec((C, W), loc),
                   pl.BlockSpec((1, H_A, DK_A, DV_A), lambda b, c: (b, 0, 0, 0))],
        out_shape=[jax.ShapeDtypeStruct((nseq * seqlen, W), F32),
                   jax.ShapeDtypeStruct((nseq, H_A, DK_A, DV_A), F32)],
        scratch_shapes=[pltpu.VMEM((H_A, DK_A, DV_A), F32)],
        compiler_params=_cparams(("parallel", "arbitrary")),
        name="delta_rule",
    )(q, k, v, bd, z, head_par, norm_w, s0)


def _sb_tile(qm, kt, vt, bias, mask, carry, su, kv_transposed=False):
    qk = _dot(qm, kt) if kv_transposed else _dot_nt(qm, kt)
    z = qk * (D_HB ** -0.5) + bias
    sp = _softplus(z)
    lk = -sp if mask is None else jnp.where(mask, -sp, 0.0)
    hi, lo = _split2(lk)
    la = (_dot(hi, su) + _dot(lo, su)) + carry
    a = jnp.exp(z - sp + la)
    if mask is not None:
        a = jnp.where(mask, a, 0.0)
    pv = _dot_nt(_bf(a), vt) if kv_transposed else _dot(_bf(a), vt)
    return pv, carry + jnp.sum(lk, axis=-1, keepdims=True)


def _strict_lower_ones(n):
    r = lax.broadcasted_iota(jnp.int32, (n, n), 0)
    c = lax.broadcasted_iota(jnp.int32, (n, n), 1)
    return jnp.where(r > c, 1.0, 0.0).astype(BF16)


def _sb_prompt_kernel(bias_ref, q_ref, k_ref, v_ref, o_ref, *, tq, tk):
    hp = pl.program_id(1)
    i = pl.program_id(2)
    lane = lax.broadcasted_iota(jnp.int32, (tq, LANE), 1)
    first = lane < D_HB
    q = q_ref[...]
    qm = (_bf(jnp.where(first, q, 0.0)), _bf(jnp.where(first, 0.0, q)))
    bias = (bias_ref[2 * hp], bias_ref[2 * hp + 1])
    su = _strict_lower_ones(tk)
    rpos = i * tq + lax.broadcasted_iota(jnp.int32, (tq, tk), 0)
    cidx = lax.broadcasted_iota(jnp.int32, (tq, tk), 1)
    nk = (i * tq + tq + tk - 1) // tk

    def body(jj, st):
        c0, c1, acc = st
        j = nk - 1 - jj
        start = pl.multiple_of(j * tk, tk)
        kt = _bf(k_ref[pl.ds(start, tk), :])
        vt = _bf(v_ref[pl.ds(start, tk), :])
        mask = (j * tk + cidx) < rpos
        pv0, c0 = _sb_tile(qm[0], kt, vt, bias[0], mask, c0, su)
        pv1, c1 = _sb_tile(qm[1], kt, vt, bias[1], mask, c1, su)
        return c0, c1, acc + jnp.where(first, pv0, pv1)

    zc = jnp.zeros((tq, 1), F32)
    _, _, acc = lax.fori_loop(0, nk, body, (zc, zc, jnp.zeros((tq, LANE), F32)))
    o_ref[...] = acc


def sb_attention_prompt(qb, kb, vb, sb_bias, nseq, seqlen, tq=128, tk=128):
    W = qb.shape[1]
    nq = seqlen // tq
    nhp = W // LANE
    kern = functools.partial(_sb_prompt_kernel, tq=tq, tk=tk)
    kv_spec = pl.BlockSpec((seqlen, LANE), lambda b, hp, i: (b, hp))
    return pl.pallas_call(
        kern,
        grid=(nseq, nhp, nq),
        in_specs=[pl.BlockSpec(memory_space=pltpu.SMEM),
                  pl.BlockSpec((tq, LANE), lambda b, hp, i: (b * nq + i, hp)),
                  kv_spec, kv_spec],
        out_specs=pl.BlockSpec((tq, LANE), lambda b, hp, i: (b * nq + i, hp)),
        out_shape=jax.ShapeDtypeStruct((nseq * seqlen, W), F32),
        compiler_params=_cparams(("parallel", "parallel", "arbitrary")),
        name="sb_attention_prompt",
    )(sb_bias, qb, kb, vb)


def _sb_sample_kernel(pt_ref, q_ref, ko_ref, vo_ref, bias_ref, *rest, n_pages, n_tok, page):
    k_refs = rest[:n_pages]
    v_refs = rest[n_pages:2 * n_pages]
    o_ref = rest[2 * n_pages]
    W = H_B * D_HB
    R = n_tok * H_B
    hrow = lax.broadcasted_iota(jnp.int32, (H_B, W), 0)
    hlane = lax.broadcasted_iota(jnp.int32, (H_B, W), 1) // D_HB
    hmask = hrow == hlane
    q = q_ref[...]
    qm = _bf(jnp.concatenate(
        [jnp.where(hmask, jnp.broadcast_to(q[t:t + 1, :], (H_B, W)), 0.0) for t in range(n_tok)],
        axis=0))
    bias = jnp.concatenate([bias_ref[...]] * n_tok, axis=0)
    no = ko_ref.shape[0]
    trow = lax.broadcasted_iota(jnp.int32, (R, no), 0) // H_B
    scol = lax.broadcasted_iota(jnp.int32, (R, no), 1)
    acc, carry = _sb_tile(qm, _bf(ko_ref[...]), _bf(vo_ref[...]), bias, scol < trow,
                          jnp.zeros((R, 1), F32), _strict_lower_ones(no))
    su = _strict_lower_ones(page)
    for p in reversed(range(n_pages)):
        pv, carry = _sb_tile(qm, _bf(k_refs[p][0]), _bf(v_refs[p][0]), bias, None, carry, su,
                             kv_transposed=True)
        acc = acc + pv
    rows = [jnp.sum(jnp.where(hmask, acc[t * H_B:(t + 1) * H_B, :], 0.0), axis=0, keepdims=True)
            for t in range(n_tok)]
    rows.append(jnp.zeros((q.shape[0] - n_tok, W), F32))
    o_ref[...] = jnp.concatenate(rows, axis=0)


def sb_attention_sample(qb, kb, vb, sb_bias_col, cache_k, cache_v, page_table_flat, layer,
                        row0, nseq, n_pages, n_tok):
    W = qb.shape[1]
    page = cache_k.shape[2]
    rb0 = row0 // SAMPLE_PAD
    own = pl.BlockSpec((SAMPLE_PAD, W), lambda b, pt: (rb0 + b, 0))

    def page_spec(p):
        return pl.BlockSpec((1, W, page), lambda b, pt: (layer + pt[b * n_pages + p], 0, 0))

    kern = functools.partial(_sb_sample_kernel, n_pages=n_pages, n_tok=n_tok, page=page)
    grid_spec = pltpu.PrefetchScalarGridSpec(
        num_scalar_prefetch=1,
        grid=(nseq,),
        in_specs=[own, own, own, pl.BlockSpec((H_B, 1), lambda b, pt: (0, 0))]
                 + [page_spec(p) for p in range(n_pages)] * 2,
        out_specs=pl.BlockSpec((SAMPLE_PAD, W), lambda b, pt: (b, 0)),
    )
    return pl.pallas_call(
        kern,
        grid_spec=grid_spec,
        out_shape=jax.ShapeDtypeStruct((nseq * SAMPLE_PAD, W), F32),
        compiler_params=_cparams(("arbitrary",)),
        name="sb_attention_sample",
    )(page_table_flat, qb, kb, vb, sb_bias_col, *([cache_k] * n_pages), *([cache_v] * n_pages))


def _merge_kernel(x_ref, oa_ref, ob_ref, gla_ref, glb_ref, wa_ref, wb_ref, wo_ref, ln_ref,
                  o_ref, *, alpha):
    pa = _dot(_bf(oa_ref[...]), wa_ref[...])
    pb = _dot(_bf(ob_ref[...]), wb_ref[...])
    m = _sigmoid(gla_ref[...]) * pa + _sigmoid(glb_ref[...]) * pb
    y = _dot(_bf(m), wo_ref[...])
    ln = ln_ref[...]
    o_ref[...] = _layer_norm(alpha * x_ref[...] + y, ln[0:1, :], ln[1:2, :])


def merge_out_ln(x, oa, ob, gla, glb, wa, wb, wo, ln, alpha, tm=256):
    T, D = x.shape
    W = oa.shape[1]
    row = lambda i: (i, 0)
    full = lambda i: (0, 0)
    return pl.pallas_call(
        functools.partial(_merge_kernel, alpha=alpha),
        grid=(T // tm,),
        in_specs=[pl.BlockSpec((tm, D), row), pl.BlockSpec((tm, W), row), pl.BlockSpec((tm, W), row),
                  pl.BlockSpec((tm, D), row), pl.BlockSpec((tm, D), row),
                  pl.BlockSpec((W, D), full), pl.BlockSpec((W, D), full), pl.BlockSpec((D, D), full),
                  pl.BlockSpec((2, D), full)],
        out_specs=pl.BlockSpec((tm, D), row),
        out_shape=jax.ShapeDtypeStruct((T, D), F32),
        compiler_params=_cparams(("parallel",)),
        name="merge_out_ln",
    )(x, oa, ob, gla, glb, wa, wb, wo, ln)


def _router_kernel(x_ref, w_ref, g_ref, *, n_exp):
    logits = _dot3(x_ref[...], w_ref[...])
    lane = lax.broadcasted_iota(jnp.int32, logits.shape, 1)
    neg = jnp.float32(-jnp.inf)
    l1 = jnp.where(lane < n_exp, logits, neg)
    m1 = jnp.max(l1, axis=-1, keepdims=True)
    i1 = jnp.min(jnp.where(l1 == m1, lane, LANE), axis=-1, keepdims=True)
    l2 = jnp.where(lane == i1, neg, l1)
    m2 = jnp.max(l2, axis=-1, keepdims=True)
    i2 = jnp.min(jnp.where(l2 == m2, lane, LANE), axis=-1, keepdims=True)
    e = jnp.exp(m2 - m1)
    w1 = 1.0 / (1.0 + e)
    g_ref[...] = jnp.where(lane == i1, w1, 0.0) + jnp.where(lane == i2, e * w1, 0.0)


def router_gate(x, router_w, tm=512):
    T, D = x.shape
    n_exp = router_w.shape[1]
    wpad = jnp.pad(router_w, ((0, 0), (0, LANE - n_exp)))
    return pl.pallas_call(
        functools.partial(_router_kernel, n_exp=n_exp),
        grid=(T // tm,),
        in_specs=[pl.BlockSpec((tm, D), lambda i: (i, 0)), pl.BlockSpec((D, LANE), lambda i: (0, 0))],
        out_specs=pl.BlockSpec((tm, LANE), lambda i: (i, 0)),
        out_shape=jax.ShapeDtypeStruct((T, LANE), F32),
        compiler_params=_cparams(("parallel",)),
        name="router_gate",
    )(x, wpad)


def _ffn_kernel(x_ref, gate_ref, wg_ref, wu_ref, wd_ref, ln_ref, o_ref, acc_ref, *, alpha, gated):
    e = pl.program_id(1)
    f = pl.program_id(2)
    last = jnp.logical_and(e == pl.num_programs(1) - 1, f == pl.num_programs(2) - 1)

    @pl.when(jnp.logical_and(e == 0, f == 0))
    def _():
        acc_ref[...] = jnp.zeros_like(acc_ref)

    x = x_ref[...]
    xb = _bf(x)
    h = _silu(_dot(xb, wg_ref[0])) * _dot(xb, wu_ref[0])
    y = _dot(_bf(h), wd_ref[0])
    if gated:
        lane = lax.broadcasted_iota(jnp.int32, gate_ref.shape, 1)
        ge = jnp.sum(jnp.where(lane == e, gate_ref[...], 0.0), axis=-1, keepdims=True)
        y = ge * y
    acc_ref[...] += y

    @pl.when(last)
    def _():
        ln = ln_ref[...]
        o_ref[...] = _layer_norm(alpha * x + acc_ref[...], ln[0:1, :], ln[1:2, :])


def ffn_ln(x, gate, wg, wu, wd, ln, alpha, gated, tm=512, tf=None):
    T, D = x.shape
    E, _, F = wg.shape
    if tf is None:
        tf = next(c for c in (512, 384, 256, 128) if F % c == 0)
    row = lambda i, e, f: (i, 0)
    return pl.pallas_call(
        functools.partial(_ffn_kernel, alpha=alpha, gated=gated),
        grid=(T // tm, E, F // tf),
        in_specs=[pl.BlockSpec((tm, D), row), pl.BlockSpec((tm, LANE), row),
                  pl.BlockSpec((1, D, tf), lambda i, e, f: (e, 0, f)),
                  pl.BlockSpec((1, D, tf), lambda i, e, f: (e, 0, f)),
                  pl.BlockSpec((1, tf, D), lambda i, e, f: (e, f, 0)),
                  pl.BlockSpec((2, D), lambda i, e, f: (0, 0))],
        out_specs=pl.BlockSpec((tm, D), row),
        out_shape=jax.ShapeDtypeStruct((T, D), F32),
        scratch_shapes=[pltpu.VMEM((tm, D), F32)],
        compiler_params=_cparams(("parallel", "arbitrary", "arbitrary")),
        name="ffn_ln",
    )(x, gate, wg, wu, wd, ln)


def kernel(x_prompt, x_sample, cache_k, cache_v, state_delta, state_conv, page_table, w_in, conv_w,
           a_log, dt_bias, o_norm_w, sb_bias, w_proj_a, w_proj_b, w_out, ln1_g, ln1_b, ln2_g, ln2_b,
           ffn_w_gate, ffn_w_up, ffn_w_down, router_w, moe_w_gate, moe_w_up, moe_w_down):
    pb, seq, D = x_prompt.shape
    db, dseq, _ = x_sample.shape
    depth = w_in.shape[0]
    n_phys, page = cache_k.shape[1], cache_k.shape[2]
    n_pages = page_table.shape[1]
    W_a = H_A * DV_A
    W_b = H_B * D_HB
    C3 = 3 * H_A * DK_A
    alpha = (2.0 * depth) ** 0.25
    tp = pb * seq
    ts = db * SAMPLE_PAD

    xs_pad = jnp.pad(x_sample, ((0, 0), (0, SAMPLE_PAD - dseq), (0, 0)))
    x = jnp.concatenate([x_prompt.reshape(tp, D), xs_pad.reshape(ts, D)], axis=0)
    ck = jnp.transpose(cache_k, (0, 1, 3, 4, 2)).reshape(depth * n_phys, W_b, page)
    cv = jnp.transpose(cache_v, (0, 1, 3, 4, 2)).reshape(depth * n_phys, W_b, page)
    pt_flat = page_table.reshape(-1).astype(jnp.int32)
    buf0 = jnp.zeros((pb, CONV_W - 1, C3), F32)
    s00 = jnp.zeros((pb, H_A, DK_A, DV_A), F32)
    gate_ones = jnp.ones((tp + ts, LANE), F32)

    outs = {n: [] for n in ("kp", "vp", "sp", "cp", "ks", "vs", "ss", "cs")}
    for li in range(depth):
        proj = in_projection(x, pack_w_in(w_in[li]))
        qkv, z, gla, glb, qb, kb, vb, bd = proj
        head_par = jnp.zeros((2, LANE), F32)
        head_par = head_par.at[0, H_A:2 * H_A].set(-jnp.exp(a_log[li].astype(F32)))
        head_par = head_par.at[1, H_A:2 * H_A].set(dt_bias[li].astype(F32))
        norm_w = o_norm_w[li].reshape(1, DV_A).astype(F32)

        qp, kp_, vp_, cbuf_p = conv_prep(qkv, conv_w[li], buf0, 0, pb, seq, 512, seq)
        qs, ks_, vs_, cbuf_s = conv_prep(qkv, conv_w[li], state_conv[li], tp, db, SAMPLE_PAD,
                                         SAMPLE_PAD, dseq)
        oa_p, s_p = delta_rule(qp, kp_, vp_, bd, z, head_par, norm_w, s00, 0, pb, seq,
                               DELTA_CHUNK, seq)
        oa_s, s_s = delta_rule(qs, ks_, vs_, bd, z, head_par, norm_w, state_delta[li], tp, db,
                               SAMPLE_PAD, SAMPLE_PAD, dseq)
        ob_p = sb_attention_prompt(qb, kb, vb, sb_bias[li].astype(F32), pb, seq)
        ob_s = sb_attention_sample(qb, kb, vb, sb_bias[li].astype(F32).reshape(H_B, 1), ck, cv,
                                   pt_flat, li * n_phys, tp, db, n_pages, dseq)
        oa = jnp.concatenate([oa_p, oa_s], axis=0)
        ob = jnp.concatenate([ob_p, ob_s], axis=0)
        ln1 = jnp.stack([ln1_g[li], ln1_b[li]]).astype(F32)
        ln2 = jnp.stack([ln2_g[li], ln2_b[li]]).astype(F32)
        x1 = merge_out_ln(x, oa, ob, gla, glb, _bf(w_proj_a[li]), _bf(w_proj_b[li]), _bf(w_out[li]),
                          ln1, alpha)
        j = li // 2
        if li % 2 == 0:
            x = ffn_ln(x1, gate_ones, _bf(ffn_w_gate[j])[None], _bf(ffn_w_up[j])[None],
                       _bf(ffn_w_down[j])[None], ln2, alpha, gated=False)
        else:
            gate = router_gate(x1, router_w[j].astype(F32))
            x = ffn_ln(x1, gate, _bf(moe_w_gate[j]), _bf(moe_w_up[j]), _bf(moe_w_down[j]), ln2,
                       alpha, gated=True)

        outs["kp"].append(kb[:tp].reshape(pb, seq, H_B, D_HB))
        outs["vp"].append(vb[:tp].reshape(pb, seq, H_B, D_HB))
        outs["sp"].append(s_p)
        outs["cp"].append(cbuf_p)
        outs["ks"].append(kb[tp:].reshape(db, SAMPLE_PAD, H_B, D_HB)[:, :dseq])
        outs["vs"].append(vb[tp:].reshape(db, SAMPLE_PAD, H_B, D_HB)[:, :dseq])
        outs["ss"].append(s_s)
        outs["cs"].append(cbuf_s)

    y_prompt = x[:tp].reshape(pb, seq, D)
    y_sample = x[tp:].reshape(db, SAMPLE_PAD, D)[:, :dseq]
    st = {n: jnp.stack(v) for n, v in outs.items()}
    return (y_prompt, y_sample, st["kp"], st["vp"], st["sp"], st["cp"],
            st["ks"], st["vs"], st["ss"], st["cs"])
```

```python
import functools
import math

import jax
import jax.numpy as jnp
from jax import lax
from jax.experimental import pallas as pl
from jax.experimental.pallas import tpu as pltpu

F32 = jnp.float32
BF16 = jnp.bfloat16

H_A = 8
DK_A = 64
DV_A = 64
H_B = 8
D_HB = 64
CONV_W = 4
DELTA_CHUNK = 64
LN_EPS = 1e-5
NORM_EPS = 1e-6
SAMPLE_PAD = 8
LANE = 128
VMEM_LIMIT = 56 * 1024 * 1024
LOG2E = 1.4426950408889634
SB_TQ = 512
SB_TK = 256


def _bf(x):
    return x.astype(BF16)


def _dot(a, b):
    return jnp.dot(a, b, preferred_element_type=F32)


def _dot_nt(a, b):
    return lax.dot_general(a, b, (((1,), (1,)), ((), ())), preferred_element_type=F32)


def _dot_tn(a, b):
    return lax.dot_general(a, b, (((0,), (0,)), ((), ())), preferred_element_type=F32)


def _split2(x):
    hi = _bf(x)
    lo = _bf(x - hi.astype(F32))
    return hi, lo


def _split3(x):
    hi = _bf(x)
    r = x - hi.astype(F32)
    mid = _bf(r)
    lo = _bf(r - mid.astype(F32))
    return hi, mid, lo


def _dot3s(a, b):
    return _dot(a[0], b[0]) + (_dot(a[0], b[1]) + _dot(a[1], b[0]))


def _dot3(a, b):
    return _dot3s(_split2(a), _split2(b))


def _silu(x):
    return x * (1.0 / (1.0 + jnp.exp(-x)))


def _sigmoid(x):
    return 1.0 / (1.0 + jnp.exp(-x))


def _softplus(x):
    return jnp.maximum(x, 0.0) + jnp.log1p(jnp.exp(-jnp.abs(x)))


def _layer_norm(x, g, b):
    mu = jnp.mean(x, axis=-1, keepdims=True)
    xc = x - mu
    var = jnp.mean(xc * xc, axis=-1, keepdims=True)
    return xc * lax.rsqrt(var + LN_EPS) * g + b


def _cparams(sem, vmem=VMEM_LIMIT):
    return pltpu.CompilerParams(dimension_semantics=sem, vmem_limit_bytes=vmem)


_PROJ_GROUPS = (("qkv", 1536, F32), ("z", 512, F32), ("gla", 1024, F32), ("glb", 1024, F32),
                ("bd", 128, F32), ("qb", 512, F32))
_PROJ_MAIN = sum(w for _, w, _ in _PROJ_GROUPS)
_W_KV = 2 * H_B * D_HB


def _proj_main_kernel(x_ref, w_ref, *o_refs):
    x = _bf(x_ref[...])
    off = 0
    for (_, width, dt), o_ref in zip(_PROJ_GROUPS, o_refs):
        o_ref[...] = _dot(x, w_ref[:, off:off + width]).astype(dt)
        off += width


def in_projection_main(x, w_main, tm=256):
    T, D = x.shape
    row = lambda i: (i, 0)
    return pl.pallas_call(
        _proj_main_kernel,
        grid=(T // tm,),
        in_specs=[pl.BlockSpec((tm, D), row), pl.BlockSpec((D, _PROJ_MAIN), lambda i: (0, 0))],
        out_specs=[pl.BlockSpec((tm, w), row) for _, w, _ in _PROJ_GROUPS],
        out_shape=[jax.ShapeDtypeStruct((T, w), dt) for _, w, dt in _PROJ_GROUPS],
        compiler_params=_cparams(("parallel",)),
        name="in_projection_main",
    )(x, w_main)


def _proj_kv_kernel(x_ref, wkv_ref, *o_refs, kv_transposed):
    x = _bf(x_ref[...])
    wb = H_B * D_HB
    if kv_transposed:
        k32, v32, kb16, vb16 = o_refs
        kt = _dot_nt(wkv_ref[:wb, :], x)
        vt = _dot_nt(wkv_ref[wb:, :], x)
        k32[0] = kt
        v32[0] = vt
        kb16[0, 0] = _bf(kt)
        vb16[0, 0] = _bf(vt)
    else:
        k32, v32 = o_refs
        k32[...] = _dot_nt(x, wkv_ref[:wb, :])
        v32[...] = _dot_nt(x, wkv_ref[wb:, :])


def in_projection_kv(x, wkv_t, row0, nrows, nseq, seqlen, kv_transposed, tm=SB_TK):
    D = x.shape[1]
    W = H_B * D_HB
    rb0 = row0 // tm
    row = lambda i: (i, 0)
    if kv_transposed:
        nt = seqlen // tm
        t32 = pl.BlockSpec((1, W, tm), lambda i: (i // nt, 0, i % nt))
        t16 = pl.BlockSpec((1, 1, W, tm), lambda i: (i // nt, i % nt, 0, 0))
        outs = [jax.ShapeDtypeStruct((nseq, W, seqlen), F32)] * 2
        outs += [jax.ShapeDtypeStruct((nseq, nt, W, tm), BF16)] * 2
        out_specs = [t32, t32, t16, t16]
    else:
        outs = [jax.ShapeDtypeStruct((nrows, W), F32)] * 2
        out_specs = [pl.BlockSpec((tm, W), row)] * 2
    return pl.pallas_call(
        functools.partial(_proj_kv_kernel, kv_transposed=kv_transposed),
        grid=(nrows // tm,),
        in_specs=[pl.BlockSpec((tm, D), lambda i: (rb0 + i, 0)),
                  pl.BlockSpec((_W_KV, D), lambda i: (0, 0))],
        out_specs=out_specs,
        out_shape=outs,
        compiler_params=_cparams(("parallel",)),
        name="in_projection_kv",
    )(x, wkv_t)


def pack_w_in(w_in):
    c = 3 * H_A * DK_A
    wv = H_A * DV_A
    wb = H_B * D_HB
    D = w_in.shape[0]
    i = 0
    qkv = w_in[:, i:i + c]; i += c
    z = w_in[:, i:i + wv]; i += wv
    bd = w_in[:, i:i + 2 * H_A]; i += 2 * H_A
    qb = w_in[:, i:i + wb]; i += wb
    kv = w_in[:, i:i + 2 * wb]; i += 2 * wb
    gla = w_in[:, i:i + D]; i += D
    glb = w_in[:, i:i + D]; i += D
    bd = jnp.pad(bd, ((0, 0), (0, LANE - 2 * H_A)))
    main = _bf(jnp.concatenate([qkv, z, gla, glb, bd, qb], axis=1))
    return main, _bf(kv.T)


def _prep_kernel(x_ref, w_ref, buf_ref, hsum_ref, q_ref, k_ref, v_ref, nb_ref, xp_ref,
                 *, tt, valid_last):
    t = pl.program_id(1)
    nt = pl.num_programs(1)
    halo = CONV_W - 1
    base = 8

    @pl.when(t == 0)
    def _():
        xp_ref[base - halo:base, :] = buf_ref[0]

    @pl.when(t > 0)
    def _():
        xp_ref[base - halo:base, :] = xp_ref[base + tt - halo:base + tt, :]

    xp_ref[base:base + tt, :] = x_ref[...]
    w = w_ref[...]
    y = xp_ref[base - halo:base - halo + tt, :] * w[0:1, :]
    for i in range(1, CONV_W):
        y = y + xp_ref[base - halo + i:base - halo + i + tt, :] * w[i:i + 1, :]
    y = _silu(y)
    wq = H_A * DK_A
    q = y[:, :wq]
    k = y[:, wq:2 * wq]
    v_ref[...] = y[:, 2 * wq:]
    hs = hsum_ref[...]

    def l2n(a):
        hi, lo = _split2(a * a)
        ss = _dot(hi, hs) + _dot(lo, hs)
        return a * lax.rsqrt(ss + NORM_EPS)

    q_ref[...] = l2n(q) * (DK_A ** -0.5)
    k_ref[...] = l2n(k)

    @pl.when(t == nt - 1)
    def _():
        nb_ref[0] = xp_ref[base + valid_last - halo:base + valid_last, :]


def conv_prep(qkv, conv_w, buf, row0, nseq, seqlen, tt, valid_len):
    C = qkv.shape[1]
    T = nseq * seqlen
    W = C // 3
    nt = seqlen // tt
    rb0 = row0 // tt
    valid_last = valid_len - (nt - 1) * tt
    head = jnp.arange(W, dtype=jnp.int32) // DK_A
    hsum = (head[:, None] == head[None, :]).astype(BF16)
    row_map = lambda b, t: (b * nt + t, 0)
    kern = functools.partial(_prep_kernel, tt=tt, valid_last=valid_last)
    return pl.pallas_call(
        kern,
        grid=(nseq, nt),
        in_specs=[pl.BlockSpec((tt, C), lambda b, t: (rb0 + b * nt + t, 0)),
                  pl.BlockSpec((CONV_W, C), lambda b, t: (0, 0)),
                  pl.BlockSpec((1, CONV_W - 1, C), lambda b, t: (b, 0, 0)),
                  pl.BlockSpec((W, W), lambda b, t: (0, 0))],
        out_specs=[pl.BlockSpec((tt, W), row_map),
                   pl.BlockSpec((tt, W), row_map),
                   pl.BlockSpec((tt, W), row_map),
                   pl.BlockSpec((1, CONV_W - 1, C), lambda b, t: (b, 0, 0))],
        out_shape=[jax.ShapeDtypeStruct((T, W), F32)] * 3
                  + [jax.ShapeDtypeStruct((nseq, CONV_W - 1, C), F32)],
        scratch_shapes=[pltpu.VMEM((tt + 8, C), F32)],
        compiler_params=_cparams(("parallel", "arbitrary")),
        name="conv_prep",
    )(qkv, conv_w, buf, hsum)


def _delta_gates(bd, hp, C, valid_len, tri_incl):
    beta_all = _sigmoid(bd)
    g_all = hp[0:1, :] * _softplus(bd + hp[1:2, :])
    if valid_len < C:
        pos = lax.broadcasted_iota(jnp.int32, bd.shape, 0) % C
        live = pos < valid_len
        beta_all = jnp.where(live, beta_all, 0.0)
        g_all = jnp.where(live, g_all, 0.0)
    gcs = []
    for g in range(bd.shape[0] // C):
        gh, gm, gl = _split3(g_all[g * C:(g + 1) * C, :])
        gcs.append(_dot(tri_incl, gh) + (_dot(tri_incl, gm) + _dot(tri_incl, gl)))
    return beta_all, gcs


def _delta_intra(qs, ks, vs, betas, gcs, C, nstage):
    n = len(qs)
    row = lax.broadcasted_iota(jnp.int32, (C, C), 0)
    col = lax.broadcasted_iota(jnp.int32, (C, C), 1)
    incl = row >= col
    strict = row > col
    eye = row == col
    ones_cc = jnp.ones((C, C), BF16)
    rng = range(n)
    a_mats = [jnp.broadcast_to(gcs[i], (C, C)) for i in rng]
    dsp = [_split3(jnp.where(eye, a_mats[i], 0.0)) for i in rng]
    b_mats = [_dot_nt(ones_cc, dsp[i][0]) + (_dot_nt(ones_cc, dsp[i][1]) + _dot_nt(ones_cc, dsp[i][2]))
              for i in rng]
    decays = [jnp.where(incl, jnp.exp(jnp.minimum(a_mats[i] - b_mats[i], 0.0)), 0.0) for i in rng]
    kbs = [ks[i] * betas[i] for i in rng]
    kks = [_dot_nt(_bf(kbs[i]), _bf(ks[i])) for i in rng]
    qks = [_dot_nt(_bf(qs[i]), _bf(ks[i])) for i in rng]
    egs = [jnp.exp(gcs[i]) for i in rng]
    ps = [_split2(-jnp.where(strict, kks[i] * decays[i], 0.0)) for i in rng]
    rs = [jnp.concatenate([vs[i] * betas[i], kbs[i] * egs[i]], axis=-1) for i in rng]
    for s in range(nstage):
        if s > 0:
            pn = [_dot3s(ps[i], ps[i]) for i in rng]
            ps = [_split2(pn[i]) for i in rng]
        rsp = [_split2(rs[i]) for i in rng]
        dr = [_dot3s(ps[i], rsp[i]) for i in rng]
        rs = [rs[i] + dr[i] for i in rng]
    us = [rs[i][:, :DV_A] for i in rng]
    ws = [rs[i][:, DV_A:] for i in rng]
    qkm = [jnp.where(incl, qks[i] * decays[i], 0.0) for i in rng]
    kdecs = [ks[i] * jnp.exp(gcs[i][C - 1:C, :] - gcs[i]) for i in rng]
    qgs = [qs[i] * egs[i] for i in rng]
    return us, ws, qkm, kdecs, qgs


def _delta_state(us, ws, qkm, kdecs, qgs, egl, states):
    rng = range(len(us))
    sb = [_bf(states[i]) for i in rng]
    ws_s = [_dot(ws[i], sb[i]) for i in rng]
    qs_s = [_dot(qgs[i], sb[i]) for i in rng]
    vn = [_bf(us[i] - ws_s[i]) for i in rng]
    os_ = [qs_s[i] + _dot(qkm[i], vn[i]) for i in rng]
    new = [states[i] * egl[i] + _dot_tn(kdecs[i], vn[i]) for i in rng]
    return os_, new


def _gated_rms(o, nw, z):
    rn = o * lax.rsqrt(jnp.mean(o * o, axis=-1, keepdims=True) + NORM_EPS)
    return rn * nw * _silu(z)


def _delta_intra_kernel(q_ref, k_ref, v_ref, bd_ref, hp_ref,
                        u_ref, w_ref, qk_ref, kd_ref, qg_ref, egl_ref, *, C, G, nstage):
    row = lax.broadcasted_iota(jnp.int32, (C, C), 0)
    col = lax.broadcasted_iota(jnp.int32, (C, C), 1)
    tri_incl = jnp.where(row >= col, 1.0, 0.0).astype(BF16)
    beta_all, gcs = _delta_gates(bd_ref[...], hp_ref[...], C, C, tri_incl)
    qs, ks, vs, betas, gch = [], [], [], [], []
    for g in range(G):
        rows = slice(g * C, (g + 1) * C)
        for h in range(H_A):
            sl = slice(h * DK_A, (h + 1) * DK_A)
            qs.append(q_ref[rows, sl])
            ks.append(k_ref[rows, sl])
            vs.append(v_ref[rows, sl])
            betas.append(beta_all[rows, h:h + 1])
            gch.append(gcs[g][:, H_A + h:H_A + h + 1])
    us, ws, qkm, kdecs, qgs = _delta_intra(qs, ks, vs, betas, gch, C, nstage)
    for g in range(G):
        rows = slice(g * C, (g + 1) * C)
        pr = slice(g * H_A, (g + 1) * H_A)
        u_ref[rows, :] = jnp.concatenate(us[pr], axis=-1)
        w_ref[rows, :] = _bf(jnp.concatenate(ws[pr], axis=-1))
        qk_ref[rows, :] = _bf(jnp.concatenate(qkm[pr], axis=-1))
        kd_ref[rows, :] = _bf(jnp.concatenate(kdecs[pr], axis=-1))
        qg_ref[rows, :] = _bf(jnp.concatenate(qgs[pr], axis=-1))
        egl_ref[0, g] = jnp.exp(gcs[g][C - 1:C, :])


def _delta_seq_kernel(u_ref, w_ref, qk_ref, kd_ref, qg_ref, egl_ref, nw_ref, s0_ref, *rest, nb):
    z_refs = rest[:nb]
    o_ref, sf_ref, s_scr = rest[nb:]
    c = pl.program_id(0)
    nc = pl.num_programs(0)

    @pl.when(c == 0)
    def _():
        s_scr[...] = s0_ref[...]

    nw = nw_ref[...]
    us, ws, qkm, kdecs, qgs, egl, states = [], [], [], [], [], [], []
    for b in range(nb):
        er = egl_ref[b, 0]
        for h in range(H_A):
            sl = slice(h * DK_A, (h + 1) * DK_A)
            us.append(u_ref[b, :, sl])
            ws.append(w_ref[b, :, sl])
            qkm.append(qk_ref[b, :, sl])
            kdecs.append(kd_ref[b, :, sl])
            qgs.append(qg_ref[b, :, sl])
            egl.append(er[:, H_A + h:H_A + h + 1])
            states.append(s_scr[b, h])
    os_, new = _delta_state(us, ws, qkm, kdecs, qgs, egl, states)
    for b in range(nb):
        outs = []
        for h in range(H_A):
            i = b * H_A + h
            s_scr[b, h] = new[i]
            outs.append(_gated_rms(os_[i], nw, z_refs[b][:, h * DV_A:(h + 1) * DV_A]))
        o_ref[b] = _bf(jnp.concatenate(outs, axis=-1))

    @pl.when(c == nc - 1)
    def _():
        sf_ref[...] = s_scr[...]


def delta_rule_prompt(q, k, v, bd, z, head_par, norm_w, s0, nseq, seqlen, C, G=4):
    T, W = q.shape
    nc = seqlen // C
    nstage = max(1, int(math.log2(C)))
    R = G * C
    ncb = nc // G
    row = lambda i: (i, 0)
    intra = pl.pallas_call(
        functools.partial(_delta_intra_kernel, C=C, G=G, nstage=nstage),
        grid=(T // R,),
        in_specs=[pl.BlockSpec((R, W), row)] * 3
                 + [pl.BlockSpec((R, LANE), row), pl.BlockSpec((2, LANE), lambda i: (0, 0))],
        out_specs=[pl.BlockSpec((R, W), row)] * 5
                  + [pl.BlockSpec((1, G, 1, LANE), lambda i: (i // ncb, i % ncb, 0, 0))],
        out_shape=[jax.ShapeDtypeStruct((T, W), F32)] + [jax.ShapeDtypeStruct((T, W), BF16)] * 4
                  + [jax.ShapeDtypeStruct((nseq, nc, 1, LANE), F32)],
        compiler_params=_cparams(("parallel",)),
        name="delta_intra",
    )
    u, w, qk, kd, qg, egl = intra(q, k, v, bd, head_par)
    r3 = lambda a: a.reshape(nseq, seqlen, W)
    blk = pl.BlockSpec((nseq, C, W), lambda c: (0, c, 0))
    st = pl.BlockSpec((nseq, H_A, DK_A, DV_A), lambda c: (0, 0, 0, 0))
    o, sf = pl.pallas_call(
        functools.partial(_delta_seq_kernel, nb=nseq),
        grid=(nc,),
        in_specs=[blk] * 5 + [pl.BlockSpec((nseq, 1, 1, LANE), lambda c: (0, c, 0, 0)),
                              pl.BlockSpec((1, DV_A), lambda c: (0, 0)), st]
                 + [pl.BlockSpec((C, W), functools.partial(lambda c, b: (b * nc + c, 0), b=b))
                    for b in range(nseq)],
        out_specs=[blk, st],
        out_shape=[jax.ShapeDtypeStruct((nseq, seqlen, W), BF16),
                   jax.ShapeDtypeStruct((nseq, H_A, DK_A, DV_A), F32)],
        scratch_shapes=[pltpu.VMEM((nseq, H_A, DK_A, DV_A), F32)],
        compiler_params=_cparams(("arbitrary",)),
        name="delta_seq",
    )(r3(u), r3(w), r3(qk), r3(kd), r3(qg), egl, norm_w, s0, *([z] * nseq))
    return o.reshape(T, W), sf


def _delta_short_kernel(q_ref, k_ref, v_ref, bd_ref, z_ref, hp_ref, nw_ref, s0_ref,
                        o_ref, sf_ref, *, C, G, valid_len, nstage):
    row = lax.broadcasted_iota(jnp.int32, (C, C), 0)
    col = lax.broadcasted_iota(jnp.int32, (C, C), 1)
    tri_incl = jnp.where(row >= col, 1.0, 0.0).astype(BF16)
    beta_all, gcs = _delta_gates(bd_ref[...], hp_ref[...], C, valid_len, tri_incl)
    qs, ks, vs, betas, gch, egl, states = [], [], [], [], [], [], []
    for g in range(G):
        rows = slice(g * C, (g + 1) * C)
        for h in range(H_A):
            sl = slice(h * DK_A, (h + 1) * DK_A)
            qs.append(q_ref[rows, sl])
            ks.append(k_ref[rows, sl])
            vs.append(v_ref[rows, sl])
            betas.append(beta_all[rows, h:h + 1])
            gc = gcs[g][:, H_A + h:H_A + h + 1]
            gch.append(gc)
            egl.append(jnp.exp(gc[C - 1:C, :]))
            states.append(s0_ref[g, h])
    us, ws, qkm, kdecs, qgs = _delta_intra(qs, ks, vs, betas, gch, C, nstage)
    os_, new = _delta_state(us, [_bf(a) for a in ws], [_bf(a) for a in qkm],
                            [_bf(a) for a in kdecs], [_bf(a) for a in qgs], egl, states)
    nw = nw_ref[...]
    for g in range(G):
        rows = slice(g * C, (g + 1) * C)
        outs = []
        for h in range(H_A):
            i = g * H_A + h
            sf_ref[g, h] = new[i]
            outs.append(_gated_rms(os_[i], nw, z_ref[rows, h * DV_A:(h + 1) * DV_A]))
        o_ref[rows, :] = _bf(jnp.concatenate(outs, axis=-1))


def delta_rule_short(q, k, v, bd, z, head_par, norm_w, s0, row0, nseq, C, valid_len, G=8):
    T, W = q.shape
    nstage = max(1, int(math.log2(C)))
    R = G * C
    rb0 = row0 // R
    row = lambda i: (i, 0)
    off = lambda i: (rb0 + i, 0)
    st = pl.BlockSpec((G, H_A, DK_A, DV_A), lambda i: (i, 0, 0, 0))
    return pl.pallas_call(
        functools.partial(_delta_short_kernel, C=C, G=G, valid_len=valid_len, nstage=nstage),
        grid=(nseq // G,),
        in_specs=[pl.BlockSpec((R, W), row)] * 3
                 + [pl.BlockSpec((R, LANE), off), pl.BlockSpec((R, W), off),
                    pl.BlockSpec((2, LANE), lambda i: (0, 0)),
                    pl.BlockSpec((1, DV_A), lambda i: (0, 0)), st],
        out_specs=[pl.BlockSpec((R, W), row), st],
        out_shape=[jax.ShapeDtypeStruct((T, W), BF16),
                   jax.ShapeDtypeStruct((nseq, H_A, DK_A, DV_A), F32)],
        compiler_params=_cparams(("parallel",)),
        name="delta_short",
    )(q, k, v, bd, z, head_par, norm_w, s0)


def _sb_prompt_kernel(bias_ref, q_ref, kt_ref, vt_ref, o_ref, *, tq, tk):
    hp = pl.program_id(1)
    i = pl.program_id(2)
    lane = lax.broadcasted_iota(jnp.int32, (tq, LANE), 1)
    first = lane < D_HB
    q = _bf(q_ref[...])
    zero = jnp.zeros_like(q)
    qm = (jnp.where(first, q, zero), jnp.where(first, zero, q))
    scale2 = (D_HB ** -0.5) * LOG2E
    bias2 = (bias_ref[2 * hp] * LOG2E, bias_ref[2 * hp + 1] * LOG2E)
    r = lax.broadcasted_iota(jnp.int32, (tk, tk + LANE), 0)
    c = lax.broadcasted_iota(jnp.int32, (tk, tk + LANE), 1)
    suffix = jnp.where(jnp.logical_or(c >= tk, r > c), -1.0, 0.0).astype(BF16)
    rpos = i * tq + lax.broadcasted_iota(jnp.int32, (tq, tk), 0)
    cidx = lax.broadcasted_iota(jnp.int32, (tq, tk), 1)
    nrep = tk // LANE
    heads = (0, 1)

    def tile(j, carry, acc, masked):
        kt = kt_ref[0, j]
        vt = vt_ref[0, j]
        zs = [_dot(qm[h], kt) * scale2 + bias2[h] for h in heads]
        sps = [jnp.maximum(zs[h], 0.0) + jnp.log(1.0 + jnp.exp2(-jnp.abs(zs[h]))) * LOG2E
               for h in heads]
        if masked:
            mask = (j * tk + cidx) < rpos
            spm = [jnp.where(mask, sps[h], 0.0) for h in heads]
        else:
            spm = sps
        cm = [_dot(_bf(spm[h]), suffix) for h in heads]
        ex = [zs[h] - sps[h] + (cm[h][:, :tk] + jnp.concatenate([carry[h]] * nrep, axis=1))
              for h in heads]
        av = [jnp.exp2(ex[h]) for h in heads]
        if masked:
            av = [jnp.where(mask, av[h], 0.0) for h in heads]
        pv = [_dot_nt(_bf(av[h]), vt) for h in heads]
        carry = tuple(carry[h] + cm[h][:, tk:] for h in heads)
        return carry, acc + jnp.where(first, pv[0], pv[1])

    zc = jnp.zeros((tq, LANE), F32)
    carry, acc = (zc, zc), zc
    nd = tq // tk
    j0 = i * nd
    for dj in reversed(range(nd)):
        carry, acc = tile(j0 + dj, carry, acc, True)

    def body(jj, st):
        carry, acc = tile(j0 - 1 - jj, (st[0], st[1]), st[2], False)
        return carry[0], carry[1], acc

    _, _, acc = lax.fori_loop(0, j0, body, (carry[0], carry[1], acc))
    o_ref[...] = _bf(acc)


def sb_attention_prompt(qb, kt16, vt16, sb_bias, nseq, seqlen, tq=SB_TQ):
    W = qb.shape[1]
    nt, tk = kt16.shape[1], kt16.shape[3]
    nq = seqlen // tq
    nhp = W // LANE
    kern = functools.partial(_sb_prompt_kernel, tq=tq, tk=tk)
    kv_spec = pl.BlockSpec((1, nt, LANE, tk), lambda b, hp, i: (b, 0, hp, 0))
    return pl.pallas_call(
        kern,
        grid=(nseq, nhp, nq),
        in_specs=[pl.BlockSpec(memory_space=pltpu.SMEM),
                  pl.BlockSpec((tq, LANE), lambda b, hp, i: (b * nq + i, hp)),
                  kv_spec, kv_spec],
        out_specs=pl.BlockSpec((tq, LANE), lambda b, hp, i: (b * nq + i, hp)),
        out_shape=jax.ShapeDtypeStruct((nseq * seqlen, W), BF16),
        compiler_params=_cparams(("parallel", "parallel", "arbitrary")),
        name="sb_attention_prompt",
    )(sb_bias, qb, kt16, vt16)


def _strict_lower_ones(n):
    r = lax.broadcasted_iota(jnp.int32, (n, n), 0)
    c = lax.broadcasted_iota(jnp.int32, (n, n), 1)
    return jnp.where(r > c, 1.0, 0.0).astype(BF16)


def _sb_sample_kernel(pt_ref, q_ref, ko_ref, vo_ref, bias_ref, *rest, n_pages, n_tok, page):
    k_refs = rest[:n_pages]
    v_refs = rest[n_pages:2 * n_pages]
    o_ref = rest[2 * n_pages]
    W = H_B * D_HB
    R = n_tok * H_B
    hrow = lax.broadcasted_iota(jnp.int32, (H_B, W), 0)
    hlane = lax.broadcasted_iota(jnp.int32, (H_B, W), 1) // D_HB
    hmask = hrow == hlane
    q = q_ref[...]
    qm = _bf(jnp.concatenate(
        [jnp.where(hmask, jnp.broadcast_to(q[t:t + 1, :], (H_B, W)), 0.0) for t in range(n_tok)],
        axis=0))
    bias2 = jnp.concatenate([bias_ref[...]] * n_tok, axis=0) * LOG2E
    scale2 = (D_HB ** -0.5) * LOG2E
    no = ko_ref.shape[0]
    trow = lax.broadcasted_iota(jnp.int32, (R, no), 0) // H_B
    scol = lax.broadcasted_iota(jnp.int32, (R, no), 1)
    own_mask = scol < trow
    order = list(reversed(range(n_pages)))
    zs = [_dot_nt(qm, _bf(ko_ref[...]))] + [_dot(qm, _bf(k_refs[p][0])) for p in order]
    zs = [z * scale2 + bias2 for z in zs]
    sps = [jnp.maximum(z, 0.0) + jnp.log(1.0 + jnp.exp2(-jnp.abs(z))) * LOG2E for z in zs]
    spb = [_bf(jnp.where(own_mask, sps[0], 0.0))] + [_bf(sp) for sp in sps[1:]]
    r = lax.broadcasted_iota(jnp.int32, (page, page + LANE), 0)
    c = lax.broadcasted_iota(jnp.int32, (page, page + LANE), 1)
    suffix = jnp.where(jnp.logical_or(c >= page, r > c), -1.0, 0.0).astype(BF16)
    own_local = _dot(spb[0], -_strict_lower_ones(no))
    own_total = _dot(spb[0], jnp.full((no, LANE), -1.0, BF16))
    cms = [_dot(sp, suffix) for sp in spb[1:]]
    carry = own_total
    exs = [zs[0] - sps[0] + own_local]
    for t, cm in enumerate(cms):
        exs.append(zs[t + 1] - sps[t + 1] + (cm[:, :page] + carry))
        carry = carry + cm[:, page:]
    avs = [jnp.exp2(ex) for ex in exs]
    acc = _dot(_bf(jnp.where(own_mask, avs[0], 0.0)), _bf(vo_ref[...]))
    for t, p in enumerate(order):
        acc = acc + _dot_nt(_bf(avs[t + 1]), _bf(v_refs[p][0]))
    rows = [jnp.sum(jnp.where(hmask, acc[t * H_B:(t + 1) * H_B, :], 0.0), axis=0, keepdims=True)
            for t in range(n_tok)]
    rows.append(jnp.zeros((q.shape[0] - n_tok, W), F32))
    o_ref[...] = jnp.concatenate(rows, axis=0)


def sb_attention_sample(qb, kb, vb, sb_bias_col, cache_k, cache_v, page_table_flat, layer,
                        row0, nseq, n_pages, n_tok):
    W = qb.shape[1]
    page = cache_k.shape[2]
    rb0 = row0 // SAMPLE_PAD
    own_q = pl.BlockSpec((SAMPLE_PAD, W), lambda b, pt: (rb0 + b, 0))
    own = pl.BlockSpec((SAMPLE_PAD, W), lambda b, pt: (b, 0))

    def page_spec(p):
        return pl.BlockSpec((1, W, page), lambda b, pt: (layer + pt[b * n_pages + p], 0, 0))

    kern = functools.partial(_sb_sample_kernel, n_pages=n_pages, n_tok=n_tok, page=page)
    grid_spec = pltpu.PrefetchScalarGridSpec(
        num_scalar_prefetch=1,
        grid=(nseq,),
        in_specs=[own_q, own, own, pl.BlockSpec((H_B, 1), lambda b, pt: (0, 0))]
                 + [page_spec(p) for p in range(n_pages)] * 2,
        out_specs=pl.BlockSpec((SAMPLE_PAD, W), lambda b, pt: (b, 0)),
    )
    return pl.pallas_call(
        kern,
        grid_spec=grid_spec,
        out_shape=jax.ShapeDtypeStruct((nseq * SAMPLE_PAD, W), F32),
        compiler_params=_cparams(("arbitrary",)),
        name="sb_attention_sample",
    )(page_table_flat, qb, kb, vb, sb_bias_col, *([cache_k] * n_pages), *([cache_v] * n_pages))


def _merge_kernel(x_ref, oa_ref, ob_ref, gla_ref, glb_ref, wa_ref, wb_ref, wo_ref, ln_ref,
                  o_ref, *, alpha):
    pa = _dot(oa_ref[...], wa_ref[...])
    pb = _dot(ob_ref[...], wb_ref[...])
    m = _sigmoid(gla_ref[...]) * pa + _sigmoid(glb_ref[...]) * pb
    y = _dot(_bf(m), wo_ref[...])
    ln = ln_ref[...]
    o_ref[...] = _layer_norm(alpha * x_ref[...] + y, ln[0:1, :], ln[1:2, :])


def merge_out_ln(x, oa, ob, gla, glb, wa, wb, wo, ln, alpha, tm=256):
    T, D = x.shape
    W = oa.shape[1]
    row = lambda i: (i, 0)
    full = lambda i: (0, 0)
    return pl.pallas_call(
        functools.partial(_merge_kernel, alpha=alpha),
        grid=(T // tm,),
        in_specs=[pl.BlockSpec((tm, D), row), pl.BlockSpec((tm, W), row), pl.BlockSpec((tm, W), row),
                  pl.BlockSpec((tm, D), row), pl.BlockSpec((tm, D), row),
                  pl.BlockSpec((W, D), full), pl.BlockSpec((W, D), full), pl.BlockSpec((D, D), full),
                  pl.BlockSpec((2, D), full)],
        out_specs=pl.BlockSpec((tm, D), row),
        out_shape=jax.ShapeDtypeStruct((T, D), F32),
        compiler_params=_cparams(("parallel",)),
        name="merge_out_ln",
    )(x, oa, ob, gla, glb, wa, wb, wo, ln)


def _router_kernel(x_ref, w_ref, g_ref, *, n_exp):
    logits = _dot3(x_ref[...], w_ref[...])
    lane = lax.broadcasted_iota(jnp.int32, logits.shape, 1)
    neg = jnp.float32(-jnp.inf)
    l1 = jnp.where(lane < n_exp, logits, neg)
    m1 = jnp.max(l1, axis=-1, keepdims=True)
    i1 = jnp.min(jnp.where(l1 == m1, lane, LANE), axis=-1, keepdims=True)
    l2 = jnp.where(lane == i1, neg, l1)
    m2 = jnp.max(l2, axis=-1, keepdims=True)
    i2 = jnp.min(jnp.where(l2 == m2, lane, LANE), axis=-1, keepdims=True)
    e = jnp.exp(m2 - m1)
    w1 = 1.0 / (1.0 + e)
    g_ref[...] = jnp.where(lane == i1, w1, 0.0) + jnp.where(lane == i2, e * w1, 0.0)


def router_gate(x, router_w, tm=512):
    T, D = x.shape
    n_exp = router_w.shape[1]
    wpad = jnp.pad(router_w, ((0, 0), (0, LANE - n_exp)))
    return pl.pallas_call(
        functools.partial(_router_kernel, n_exp=n_exp),
        grid=(T // tm,),
        in_specs=[pl.BlockSpec((tm, D), lambda i: (i, 0)), pl.BlockSpec((D, LANE), lambda i: (0, 0))],
        out_specs=pl.BlockSpec((tm, LANE), lambda i: (i, 0)),
        out_shape=jax.ShapeDtypeStruct((T, LANE), F32),
        compiler_params=_cparams(("parallel",)),
        name="router_gate",
    )(x, wpad)


def _ffn_kernel(x_ref, gate_ref, wg_ref, wu_ref, wd_ref, ln_ref, o_ref, acc_ref, *, alpha, gated):
    e = pl.program_id(1)
    f = pl.program_id(2)
    last = jnp.logical_and(e == pl.num_programs(1) - 1, f == pl.num_programs(2) - 1)

    @pl.when(jnp.logical_and(e == 0, f == 0))
    def _():
        acc_ref[...] = jnp.zeros_like(acc_ref)

    x = x_ref[...]
    xb = _bf(x)
    h = _silu(_dot(xb, wg_ref[0])) * _dot(xb, wu_ref[0])
    y = _dot(_bf(h), wd_ref[0])
    if gated:
        lane = lax.broadcasted_iota(jnp.int32, gate_ref.shape, 1)
        ge = jnp.sum(jnp.where(lane == e, gate_ref[...], 0.0), axis=-1, keepdims=True)
        y = ge * y
    acc_ref[...] += y

    @pl.when(last)
    def _():
        ln = ln_ref[...]
        o_ref[...] = _layer_norm(alpha * x + acc_ref[...], ln[0:1, :], ln[1:2, :])


def ffn_ln(x, gate, wg, wu, wd, ln, alpha, gated, tm=512, tf=None):
    T, D = x.shape
    E, _, F = wg.shape
    if tf is None:
        tf = next(c for c in (512, 384, 256, 128) if F % c == 0)
    row = lambda i, e, f: (i, 0)
    return pl.pallas_call(
        functools.partial(_ffn_kernel, alpha=alpha, gated=gated),
        grid=(T // tm, E, F // tf),
        in_specs=[pl.BlockSpec((tm, D), row), pl.BlockSpec((tm, LANE), row),
                  pl.BlockSpec((1, D, tf), lambda i, e, f: (e, 0, f)),
                  pl.BlockSpec((1, D, tf), lambda i, e, f: (e, 0, f)),
                  pl.BlockSpec((1, tf, D), lambda i, e, f: (e, f, 0)),
                  pl.BlockSpec((2, D), lambda i, e, f: (0, 0))],
        out_specs=pl.BlockSpec((tm, D), row),
        out_shape=jax.ShapeDtypeStruct((T, D), F32),
        scratch_shapes=[pltpu.VMEM((tm, D), F32)],
        compiler_params=_cparams(("parallel", "arbitrary", "arbitrary")),
        name="ffn_ln",
    )(x, gate, wg, wu, wd, ln)


def kernel(x_prompt, x_sample, cache_k, cache_v, state_delta, state_conv, page_table, w_in, conv_w,
           a_log, dt_bias, o_norm_w, sb_bias, w_proj_a, w_proj_b, w_out, ln1_g, ln1_b, ln2_g, ln2_b,
           ffn_w_gate, ffn_w_up, ffn_w_down, router_w, moe_w_gate, moe_w_up, moe_w_down):
    pb, seq, D = x_prompt.shape
    db, dseq, _ = x_sample.shape
    depth = w_in.shape[0]
    n_phys, page = cache_k.shape[1], cache_k.shape[2]
    n_pages = page_table.shape[1]
    W_b = H_B * D_HB
    C3 = 3 * H_A * DK_A
    alpha = (2.0 * depth) ** 0.25
    tp = pb * seq
    ts = db * SAMPLE_PAD

    xs_pad = jnp.pad(x_sample, ((0, 0), (0, SAMPLE_PAD - dseq), (0, 0)))
    x = jnp.concatenate([x_prompt.reshape(tp, D), xs_pad.reshape(ts, D)], axis=0)
    ck = jnp.transpose(cache_k, (0, 1, 3, 4, 2)).reshape(depth * n_phys, W_b, page)
    cv = jnp.transpose(cache_v, (0, 1, 3, 4, 2)).reshape(depth * n_phys, W_b, page)
    pt_flat = page_table.reshape(-1).astype(jnp.int32)
    buf0 = jnp.zeros((pb, CONV_W - 1, C3), F32)
    s00 = jnp.zeros((pb, H_A, DK_A, DV_A), F32)
    gate_ones = jnp.ones((tp + ts, LANE), F32)

    outs = {n: [] for n in ("kp", "vp", "sp", "cp", "ks", "vs", "ss", "cs")}
    for li in range(depth):
        w_main, wkv_t = pack_w_in(w_in[li])
        qkv, z, gla, glb, bd, qb = in_projection_main(x, w_main)
        kt32, vt32, kt16, vt16 = in_projection_kv(x, wkv_t, 0, tp, pb, seq, True)
        kb_s, vb_s = in_projection_kv(x, wkv_t, tp, ts, db, SAMPLE_PAD, False)
        head_par = jnp.zeros((2, LANE), F32)
        head_par = head_par.at[0, H_A:2 * H_A].set(-jnp.exp(a_log[li].astype(F32)))
        head_par = head_par.at[1, H_A:2 * H_A].set(dt_bias[li].astype(F32))
        norm_w = o_norm_w[li].reshape(1, DV_A).astype(F32)

        qp, kp_, vp_, cbuf_p = conv_prep(qkv, conv_w[li], buf0, 0, pb, seq, 512, seq)
        qs, ks_, vs_, cbuf_s = conv_prep(qkv, conv_w[li], state_conv[li], tp, db, SAMPLE_PAD,
                                         SAMPLE_PAD, dseq)
        oa_p, s_p = delta_rule_prompt(qp, kp_, vp_, bd, z, head_par, norm_w, s00, pb, seq,
                                      DELTA_CHUNK)
        oa_s, s_s = delta_rule_short(qs, ks_, vs_, bd, z, head_par, norm_w, state_delta[li],
                                     tp, db, SAMPLE_PAD, dseq)
        ob_p = sb_attention_prompt(qb, kt16, vt16, sb_bias[li].astype(F32), pb, seq)
        ob_s = sb_attention_sample(qb, kb_s, vb_s, sb_bias[li].astype(F32).reshape(H_B, 1), ck, cv,
                                   pt_flat, li * n_phys, tp, db, n_pages, dseq)
        oa = jnp.concatenate([oa_p, oa_s], axis=0)
        ob = jnp.concatenate([ob_p, _bf(ob_s)], axis=0)
        ln1 = jnp.stack([ln1_g[li], ln1_b[li]]).astype(F32)
        ln2 = jnp.stack([ln2_g[li], ln2_b[li]]).astype(F32)
        x1 = merge_out_ln(x, oa, ob, gla, glb, _bf(w_proj_a[li]), _bf(w_proj_b[li]), _bf(w_out[li]),
                          ln1, alpha)
        j = li // 2
        if li % 2 == 0:
            x = ffn_ln(x1, gate_ones, _bf(ffn_w_gate[j])[None], _bf(ffn_w_up[j])[None],
                       _bf(ffn_w_down[j])[None], ln2, alpha, gated=False)
        else:
            gate = router_gate(x1, router_w[j].astype(F32))
            x = ffn_ln(x1, gate, _bf(moe_w_gate[j]), _bf(moe_w_up[j]), _bf(moe_w_down[j]), ln2,
                       alpha, gated=True)

        outs["kp"].append(jnp.transpose(kt32.reshape(pb, H_B, D_HB, seq), (0, 3, 1, 2)))
        outs["vp"].append(jnp.transpose(vt32.reshape(pb, H_B, D_HB, seq), (0, 3, 1, 2)))
        outs["sp"].append(s_p)
        outs["cp"].append(cbuf_p)
        outs["ks"].append(kb_s.reshape(db, SAMPLE_PAD, H_B, D_HB)[:, :dseq])
        outs["vs"].append(vb_s.reshape(db, SAMPLE_PAD, H_B, D_HB)[:, :dseq])
        outs["ss"].append(s_s)
        outs["cs"].append(cbuf_s)

    y_prompt = x[:tp].reshape(pb, seq, D)
    y_sample = x[tp:].reshape(db, SAMPLE_PAD, D)[:, :dseq]
    st = {n: jnp.stack(v) for n, v in outs.items()}
    return (y_prompt, y_sample, st["kp"], st["vp"], st["sp"], st["cp"],
            st["ks"], st["vs"], st["ss"], st["cs"])
```

```python
import functools
import math

import jax
import jax.numpy as jnp
from jax import lax
from jax.experimental import pallas as pl
from jax.experimental.pallas import tpu as pltpu

F32 = jnp.float32
BF16 = jnp.bfloat16

H_A = 8
DK_A = 64
DV_A = 64
H_B = 8
D_HB = 64
CONV_W = 4
DELTA_CHUNK = 64
LN_EPS = 1e-5
NORM_EPS = 1e-6
SAMPLE_PAD = 8
LANE = 128
VMEM_LIMIT = 56 * 1024 * 1024
LOG2E = 1.4426950408889634
SB_TQ = 512
SB_TK = 256


def _bf(x):
    return x.astype(BF16)


def _dot(a, b):
    return jnp.dot(a, b, preferred_element_type=F32)


def _dot_nt(a, b):
    return lax.dot_general(a, b, (((1,), (1,)), ((), ())), preferred_element_type=F32)


def _dot_tn(a, b):
    return lax.dot_general(a, b, (((0,), (0,)), ((), ())), preferred_element_type=F32)


def _split2(x):
    hi = _bf(x)
    lo = _bf(x - hi.astype(F32))
    return hi, lo


def _split3(x):
    hi = _bf(x)
    r = x - hi.astype(F32)
    mid = _bf(r)
    lo = _bf(r - mid.astype(F32))
    return hi, mid, lo


def _dot3s(a, b):
    return _dot(a[0], b[0]) + (_dot(a[0], b[1]) + _dot(a[1], b[0]))


def _dot3(a, b):
    return _dot3s(_split2(a), _split2(b))


def _silu(x):
    return x * (1.0 / (1.0 + jnp.exp(-x)))


def _sigmoid(x):
    return 1.0 / (1.0 + jnp.exp(-x))


def _softplus(x):
    return jnp.maximum(x, 0.0) + jnp.log1p(jnp.exp(-jnp.abs(x)))


def _layer_norm(x, g, b):
    mu = jnp.mean(x, axis=-1, keepdims=True)
    xc = x - mu
    var = jnp.mean(xc * xc, axis=-1, keepdims=True)
    return xc * lax.rsqrt(var + LN_EPS) * g + b


def _cparams(sem, vmem=VMEM_LIMIT):
    return pltpu.CompilerParams(dimension_semantics=sem, vmem_limit_bytes=vmem)


_PROJ_GROUPS = (("qkv", 1536, F32), ("z", 512, F32), ("gla", 1024, F32), ("glb", 1024, F32),
                ("bd", 128, F32), ("qb", 512, F32))
_PROJ_MAIN = sum(w for _, w, _ in _PROJ_GROUPS)
_W_KV = 2 * H_B * D_HB


def _proj_main_kernel(x_ref, w_ref, *o_refs):
    x = _bf(x_ref[...])
    off = 0
    for (_, width, dt), o_ref in zip(_PROJ_GROUPS, o_refs):
        o_ref[...] = _dot(x, w_ref[:, off:off + width]).astype(dt)
        off += width


def in_projection_main(x, w_main, tm=256):
    T, D = x.shape
    row = lambda i: (i, 0)
    return pl.pallas_call(
        _proj_main_kernel,
        grid=(T // tm,),
        in_specs=[pl.BlockSpec((tm, D), row), pl.BlockSpec((D, _PROJ_MAIN), lambda i: (0, 0))],
        out_specs=[pl.BlockSpec((tm, w), row) for _, w, _ in _PROJ_GROUPS],
        out_shape=[jax.ShapeDtypeStruct((T, w), dt) for _, w, dt in _PROJ_GROUPS],
        compiler_params=_cparams(("parallel",)),
        name="in_projection_main",
    )(x, w_main)


def _proj_kv_kernel(x_ref, wkv_ref, *o_refs, kv_transposed):
    x = _bf(x_ref[...])
    wb = H_B * D_HB
    if kv_transposed:
        k32, v32, kb16, vb16 = o_refs
        kt = _dot_nt(wkv_ref[:wb, :], x)
        vt = _dot_nt(wkv_ref[wb:, :], x)
        k32[0] = kt
        v32[0] = vt
        kb16[0, 0] = _bf(kt)
        vb16[0, 0] = _bf(vt)
    else:
        k32, v32 = o_refs
        k32[...] = _dot_nt(x, wkv_ref[:wb, :])
        v32[...] = _dot_nt(x, wkv_ref[wb:, :])


def in_projection_kv(x, wkv_t, row0, nrows, nseq, seqlen, kv_transposed, tm=SB_TK):
    D = x.shape[1]
    W = H_B * D_HB
    rb0 = row0 // tm
    row = lambda i: (i, 0)
    if kv_transposed:
        nt = seqlen // tm
        t32 = pl.BlockSpec((1, W, tm), lambda i: (i // nt, 0, i % nt))
        t16 = pl.BlockSpec((1, 1, W, tm), lambda i: (i // nt, i % nt, 0, 0))
        outs = [jax.ShapeDtypeStruct((nseq, W, seqlen), F32)] * 2
        outs += [jax.ShapeDtypeStruct((nseq, nt, W, tm), BF16)] * 2
        out_specs = [t32, t32, t16, t16]
    else:
        outs = [jax.ShapeDtypeStruct((nrows, W), F32)] * 2
        out_specs = [pl.BlockSpec((tm, W), row)] * 2
    return pl.pallas_call(
        functools.partial(_proj_kv_kernel, kv_transposed=kv_transposed),
        grid=(nrows // tm,),
        in_specs=[pl.BlockSpec((tm, D), lambda i: (rb0 + i, 0)),
                  pl.BlockSpec((_W_KV, D), lambda i: (0, 0))],
        out_specs=out_specs,
        out_shape=outs,
        compiler_params=_cparams(("parallel",)),
        name="in_projection_kv",
    )(x, wkv_t)


def pack_w_in(w_in):
    c = 3 * H_A * DK_A
    wv = H_A * DV_A
    wb = H_B * D_HB
    D = w_in.shape[0]
    i = 0
    qkv = w_in[:, i:i + c]; i += c
    z = w_in[:, i:i + wv]; i += wv
    bd = w_in[:, i:i + 2 * H_A]; i += 2 * H_A
    qb = w_in[:, i:i + wb]; i += wb
    kv = w_in[:, i:i + 2 * wb]; i += 2 * wb
    gla = w_in[:, i:i + D]; i += D
    glb = w_in[:, i:i + D]; i += D
    bd = jnp.pad(bd, ((0, 0), (0, LANE - 2 * H_A)))
    main = _bf(jnp.concatenate([qkv, z, gla, glb, bd, qb], axis=1))
    return main, _bf(kv.T)


def _prep_kernel(x_ref, w_ref, buf_ref, hsum_ref, q_ref, k_ref, v_ref, nb_ref, xp_ref,
                 *, tt, valid_last, G):
    t = pl.program_id(1)
    nt = pl.num_programs(1)
    halo = CONV_W - 1
    span = tt + 8
    w = w_ref[...]
    ys = []
    for g in range(G):
        base = g * span + 8

        @pl.when(t == 0)
        def _():
            xp_ref[base - halo:base, :] = buf_ref[g]

        if G == 1:
            @pl.when(t > 0)
            def _():
                xp_ref[base - halo:base, :] = xp_ref[base + tt - halo:base + tt, :]

        xp_ref[base:base + tt, :] = x_ref[g * tt:(g + 1) * tt, :]
        y = xp_ref[base - halo:base - halo + tt, :] * w[0:1, :]
        for i in range(1, CONV_W):
            y = y + xp_ref[base - halo + i:base - halo + i + tt, :] * w[i:i + 1, :]
        ys.append(y)
    y = _silu(ys[0] if G == 1 else jnp.concatenate(ys, axis=0))
    wq = H_A * DK_A
    q = y[:, :wq]
    k = y[:, wq:2 * wq]
    v_ref[...] = y[:, 2 * wq:]
    hs = hsum_ref[...]

    def l2n(a):
        hi, lo = _split2(a * a)
        ss = _dot(hi, hs) + _dot(lo, hs)
        return a * lax.rsqrt(ss + NORM_EPS)

    q_ref[...] = l2n(q) * (DK_A ** -0.5)
    k_ref[...] = l2n(k)

    @pl.when(t == nt - 1)
    def _():
        for g in range(G):
            base = g * span + 8
            nb_ref[g] = xp_ref[base + valid_last - halo:base + valid_last, :]


def conv_prep(qkv, conv_w, buf, row0, nseq, seqlen, tt, valid_len, G=1):
    C = qkv.shape[1]
    T = nseq * seqlen
    W = C // 3
    nt = seqlen // tt
    assert G == 1 or nt == 1
    R = G * tt
    rb0 = row0 // R
    valid_last = valid_len - (nt - 1) * tt
    head = jnp.arange(W, dtype=jnp.int32) // DK_A
    hsum = (head[:, None] == head[None, :]).astype(BF16)
    row_map = lambda b, t: (b * nt + t, 0)
    kern = functools.partial(_prep_kernel, tt=tt, valid_last=valid_last, G=G)
    return pl.pallas_call(
        kern,
        grid=(nseq // G, nt),
        in_specs=[pl.BlockSpec((R, C), lambda b, t: (rb0 + b * nt + t, 0)),
                  pl.BlockSpec((CONV_W, C), lambda b, t: (0, 0)),
                  pl.BlockSpec((G, CONV_W - 1, C), lambda b, t: (b, 0, 0)),
                  pl.BlockSpec((W, W), lambda b, t: (0, 0))],
        out_specs=[pl.BlockSpec((R, W), row_map),
                   pl.BlockSpec((R, W), row_map),
                   pl.BlockSpec((R, W), row_map),
                   pl.BlockSpec((G, CONV_W - 1, C), lambda b, t: (b, 0, 0))],
        out_shape=[jax.ShapeDtypeStruct((T, W), F32)] * 3
                  + [jax.ShapeDtypeStruct((nseq, CONV_W - 1, C), F32)],
        scratch_shapes=[pltpu.VMEM((G * (tt + 8), C), F32)],
        compiler_params=_cparams(("parallel", "arbitrary")),
        name="conv_prep",
    )(qkv, conv_w, buf, hsum)


def _delta_gates(bd, hp, C, valid_len, tri_incl):
    beta_all = _sigmoid(bd)
    g_all = hp[0:1, :] * _softplus(bd + hp[1:2, :])
    if valid_len < C:
        pos = lax.broadcasted_iota(jnp.int32, bd.shape, 0) % C
        live = pos < valid_len
        beta_all = jnp.where(live, beta_all, 0.0)
        g_all = jnp.where(live, g_all, 0.0)
    gcs = []
    for g in range(bd.shape[0] // C):
        gh, gm, gl = _split3(g_all[g * C:(g + 1) * C, :])
        gcs.append(_dot(tri_incl, gh) + (_dot(tri_incl, gm) + _dot(tri_incl, gl)))
    return beta_all, gcs


def _delta_intra(qs, ks, vs, betas, gcs, C, nstage):
    n = len(qs)
    row = lax.broadcasted_iota(jnp.int32, (C, C), 0)
    col = lax.broadcasted_iota(jnp.int32, (C, C), 1)
    incl = row >= col
    strict = row > col
    eye = row == col
    ones_cc = jnp.ones((C, C), BF16)
    rng = range(n)
    a_mats = [jnp.broadcast_to(gcs[i], (C, C)) for i in rng]
    dsp = [_split3(jnp.where(eye, a_mats[i], 0.0)) for i in rng]
    b_mats = [_dot_nt(ones_cc, dsp[i][0]) + (_dot_nt(ones_cc, dsp[i][1]) + _dot_nt(ones_cc, dsp[i][2]))
              for i in rng]
    decays = [jnp.where(incl, jnp.exp(jnp.minimum(a_mats[i] - b_mats[i], 0.0)), 0.0) for i in rng]
    kbs = [ks[i] * betas[i] for i in rng]
    kks = [_dot_nt(_bf(kbs[i]), _bf(ks[i])) for i in rng]
    qks = [_dot_nt(_bf(qs[i]), _bf(ks[i])) for i in rng]
    egs = [jnp.exp(gcs[i]) for i in rng]
    ps = [_split2(-jnp.where(strict, kks[i] * decays[i], 0.0)) for i in rng]
    rs = [jnp.concatenate([vs[i] * betas[i], kbs[i] * egs[i]], axis=-1) for i in rng]
    for s in range(nstage):
        if s > 0:
            pn = [_dot3s(ps[i], ps[i]) for i in rng]
            ps = [_split2(pn[i]) for i in rng]
        rsp = [_split2(rs[i]) for i in rng]
        dr = [_dot3s(ps[i], rsp[i]) for i in rng]
        rs = [rs[i] + dr[i] for i in rng]
    us = [rs[i][:, :DV_A] for i in rng]
    ws = [rs[i][:, DV_A:] for i in rng]
    qkm = [jnp.where(incl, qks[i] * decays[i], 0.0) for i in rng]
    kdecs = [ks[i] * jnp.exp(gcs[i][C - 1:C, :] - gcs[i]) for i in rng]
    qgs = [qs[i] * egs[i] for i in rng]
    return us, ws, qkm, kdecs, qgs


def _delta_state(us, ws, qkm, kdecs, qgs, egl, states):
    rng = range(len(us))
    sb = [_bf(states[i]) for i in rng]
    ws_s = [_dot(ws[i], sb[i]) for i in rng]
    qs_s = [_dot(qgs[i], sb[i]) for i in rng]
    vn = [_bf(us[i] - ws_s[i]) for i in rng]
    os_ = [qs_s[i] + _dot(qkm[i], vn[i]) for i in rng]
    new = [states[i] * egl[i] + _dot_tn(kdecs[i], vn[i]) for i in rng]
    return os_, new


def _gated_rms(o, nw, z):
    rn = o * lax.rsqrt(jnp.mean(o * o, axis=-1, keepdims=True) + NORM_EPS)
    return rn * nw * _silu(z)


def _delta_intra_kernel(q_ref, k_ref, v_ref, bd_ref, hp_ref,
                        u_ref, w_ref, qk_ref, kd_ref, qg_ref, egl_ref, *, C, G, nstage):
    row = lax.broadcasted_iota(jnp.int32, (C, C), 0)
    col = lax.broadcasted_iota(jnp.int32, (C, C), 1)
    tri_incl = jnp.where(row >= col, 1.0, 0.0).astype(BF16)
    beta_all, gcs = _delta_gates(bd_ref[...], hp_ref[...], C, C, tri_incl)
    qs, ks, vs, betas, gch = [], [], [], [], []
    for g in range(G):
        rows = slice(g * C, (g + 1) * C)
        for h in range(H_A):
            sl = slice(h * DK_A, (h + 1) * DK_A)
            qs.append(q_ref[rows, sl])
            ks.append(k_ref[rows, sl])
            vs.append(v_ref[rows, sl])
            betas.append(beta_all[rows, h:h + 1])
            gch.append(gcs[g][:, H_A + h:H_A + h + 1])
    us, ws, qkm, kdecs, qgs = _delta_intra(qs, ks, vs, betas, gch, C, nstage)
    for g in range(G):
        rows = slice(g * C, (g + 1) * C)
        pr = slice(g * H_A, (g + 1) * H_A)
        u_ref[rows, :] = jnp.concatenate(us[pr], axis=-1)
        w_ref[rows, :] = _bf(jnp.concatenate(ws[pr], axis=-1))
        qk_ref[rows, :] = _bf(jnp.concatenate(qkm[pr], axis=-1))
        kd_ref[rows, :] = _bf(jnp.concatenate(kdecs[pr], axis=-1))
        qg_ref[rows, :] = _bf(jnp.concatenate(qgs[pr], axis=-1))
        egl_ref[0, g] = jnp.exp(gcs[g][C - 1:C, :])


def _delta_seq_kernel(u_ref, w_ref, qk_ref, kd_ref, qg_ref, egl_ref, nw_ref, s0_ref, *rest, nb):
    z_refs = rest[:nb]
    o_ref, sf_ref, s_scr = rest[nb:]
    c = pl.program_id(0)
    nc = pl.num_programs(0)

    @pl.when(c == 0)
    def _():
        s_scr[...] = s0_ref[...]

    nw = nw_ref[...]
    us, ws, qkm, kdecs, qgs, egl, states = [], [], [], [], [], [], []
    for b in range(nb):
        er = egl_ref[b, 0]
        for h in range(H_A):
            sl = slice(h * DK_A, (h + 1) * DK_A)
            us.append(u_ref[b, :, sl])
            ws.append(w_ref[b, :, sl])
            qkm.append(qk_ref[b, :, sl])
            kdecs.append(kd_ref[b, :, sl])
            qgs.append(qg_ref[b, :, sl])
            egl.append(er[:, H_A + h:H_A + h + 1])
            states.append(s_scr[b, h])
    os_, new = _delta_state(us, ws, qkm, kdecs, qgs, egl, states)
    for b in range(nb):
        outs = []
        for h in range(H_A):
            i = b * H_A + h
            s_scr[b, h] = new[i]
            outs.append(_gated_rms(os_[i], nw, z_refs[b][:, h * DV_A:(h + 1) * DV_A]))
        o_ref[b] = _bf(jnp.concatenate(outs, axis=-1))

    @pl.when(c == nc - 1)
    def _():
        sf_ref[...] = s_scr[...]


def delta_rule_prompt(q, k, v, bd, z, head_par, norm_w, s0, nseq, seqlen, C, G=4):
    T, W = q.shape
    nc = seqlen // C
    nstage = max(1, int(math.log2(C)))
    R = G * C
    ncb = nc // G
    row = lambda i: (i, 0)
    intra = pl.pallas_call(
        functools.partial(_delta_intra_kernel, C=C, G=G, nstage=nstage),
        grid=(T // R,),
        in_specs=[pl.BlockSpec((R, W), row)] * 3
                 + [pl.BlockSpec((R, LANE), row), pl.BlockSpec((2, LANE), lambda i: (0, 0))],
        out_specs=[pl.BlockSpec((R, W), row)] * 5
                  + [pl.BlockSpec((1, G, 1, LANE), lambda i: (i // ncb, i % ncb, 0, 0))],
        out_shape=[jax.ShapeDtypeStruct((T, W), F32)] + [jax.ShapeDtypeStruct((T, W), BF16)] * 4
                  + [jax.ShapeDtypeStruct((nseq, nc, 1, LANE), F32)],
        compiler_params=_cparams(("parallel",)),
        name="delta_intra",
    )
    u, w, qk, kd, qg, egl = intra(q, k, v, bd, head_par)
    r3 = lambda a: a.reshape(nseq, seqlen, W)
    blk = pl.BlockSpec((nseq, C, W), lambda c: (0, c, 0))
    st = pl.BlockSpec((nseq, H_A, DK_A, DV_A), lambda c: (0, 0, 0, 0))
    o, sf = pl.pallas_call(
        functools.partial(_delta_seq_kernel, nb=nseq),
        grid=(nc,),
        in_specs=[blk] * 5 + [pl.BlockSpec((nseq, 1, 1, LANE), lambda c: (0, c, 0, 0)),
                              pl.BlockSpec((1, DV_A), lambda c: (0, 0)), st]
                 + [pl.BlockSpec((C, W), functools.partial(lambda c, b: (b * nc + c, 0), b=b))
                    for b in range(nseq)],
        out_specs=[blk, st],
        out_shape=[jax.ShapeDtypeStruct((nseq, seqlen, W), BF16),
                   jax.ShapeDtypeStruct((nseq, H_A, DK_A, DV_A), F32)],
        scratch_shapes=[pltpu.VMEM((nseq, H_A, DK_A, DV_A), F32)],
        compiler_params=_cparams(("arbitrary",)),
        name="delta_seq",
    )(r3(u), r3(w), r3(qk), r3(kd), r3(qg), egl, norm_w, s0, *([z] * nseq))
    return o.reshape(T, W), sf


def _delta_short_kernel(q_ref, k_ref, v_ref, bd_ref, z_ref, hp_ref, nw_ref, s0_ref,
                        o_ref, sf_ref, *, C, G, valid_len, nstage):
    row = lax.broadcasted_iota(jnp.int32, (C, C), 0)
    col = lax.broadcasted_iota(jnp.int32, (C, C), 1)
    tri_incl = jnp.where(row >= col, 1.0, 0.0).astype(BF16)
    beta_all, gcs = _delta_gates(bd_ref[...], hp_ref[...], C, valid_len, tri_incl)
    qs, ks, vs, betas, gch, egl, states = [], [], [], [], [], [], []
    for g in range(G):
        rows = slice(g * C, (g + 1) * C)
        for h in range(H_A):
            sl = slice(h * DK_A, (h + 1) * DK_A)
            qs.append(q_ref[rows, sl])
            ks.append(k_ref[rows, sl])
            vs.append(v_ref[rows, sl])
            betas.append(beta_all[rows, h:h + 1])
            gc = gcs[g][:, H_A + h:H_A + h + 1]
            gch.append(gc)
            egl.append(jnp.exp(gc[C - 1:C, :]))
            states.append(s0_ref[g, h])
    us, ws, qkm, kdecs, qgs = _delta_intra(qs, ks, vs, betas, gch, C, nstage)
    os_, new = _delta_state(us, [_bf(a) for a in ws], [_bf(a) for a in qkm],
                            [_bf(a) for a in kdecs], [_bf(a) for a in qgs], egl, states)
    nw = nw_ref[...]
    for g in range(G):
        rows = slice(g * C, (g + 1) * C)
        outs = []
        for h in range(H_A):
            i = g * H_A + h
            sf_ref[g, h] = new[i]
            outs.append(_gated_rms(os_[i], nw, z_ref[rows, h * DV_A:(h + 1) * DV_A]))
        o_ref[rows, :] = _bf(jnp.concatenate(outs, axis=-1))


def delta_rule_short(q, k, v, bd, z, head_par, norm_w, s0, row0, nseq, C, valid_len, G=8):
    T, W = q.shape
    nstage = max(1, int(math.log2(C)))
    R = G * C
    rb0 = row0 // R
    row = lambda i: (i, 0)
    off = lambda i: (rb0 + i, 0)
    st = pl.BlockSpec((G, H_A, DK_A, DV_A), lambda i: (i, 0, 0, 0))
    return pl.pallas_call(
        functools.partial(_delta_short_kernel, C=C, G=G, valid_len=valid_len, nstage=nstage),
        grid=(nseq // G,),
        in_specs=[pl.BlockSpec((R, W), row)] * 3
                 + [pl.BlockSpec((R, LANE), off), pl.BlockSpec((R, W), off),
                    pl.BlockSpec((2, LANE), lambda i: (0, 0)),
                    pl.BlockSpec((1, DV_A), lambda i: (0, 0)), st],
        out_specs=[pl.BlockSpec((R, W), row), st],
        out_shape=[jax.ShapeDtypeStruct((T, W), BF16),
                   jax.ShapeDtypeStruct((nseq, H_A, DK_A, DV_A), F32)],
        compiler_params=_cparams(("parallel",)),
        name="delta_short",
    )(q, k, v, bd, z, head_par, norm_w, s0)


def _sb_prompt_kernel(bias_ref, q_ref, kt_ref, vt_ref, o_ref, *, tq, tk):
    hp = pl.program_id(1)
    i = pl.program_id(2)
    lane = lax.broadcasted_iota(jnp.int32, (tq, LANE), 1)
    first = lane < D_HB
    q = _bf(q_ref[...])
    zero = jnp.zeros_like(q)
    qm = (jnp.where(first, q, zero), jnp.where(first, zero, q))
    scale2 = (D_HB ** -0.5) * LOG2E
    bias2 = (bias_ref[2 * hp] * LOG2E, bias_ref[2 * hp + 1] * LOG2E)
    r = lax.broadcasted_iota(jnp.int32, (tk, tk + LANE), 0)
    c = lax.broadcasted_iota(jnp.int32, (tk, tk + LANE), 1)
    suffix = jnp.where(jnp.logical_or(c >= tk, r > c), -1.0, 0.0).astype(BF16)
    rpos = i * tq + lax.broadcasted_iota(jnp.int32, (tq, tk), 0)
    cidx = lax.broadcasted_iota(jnp.int32, (tq, tk), 1)
    nrep = tk // LANE
    heads = (0, 1)

    def tile(j, carry, acc, masked):
        kt = kt_ref[0, j]
        vt = vt_ref[0, j]
        zs = [_dot(qm[h], kt) * scale2 + bias2[h] for h in heads]
        sps = [jnp.maximum(zs[h], 0.0) + jnp.log(1.0 + jnp.exp2(-jnp.abs(zs[h]))) * LOG2E
               for h in heads]
        if masked:
            mask = (j * tk + cidx) < rpos
            spm = [jnp.where(mask, sps[h], 0.0) for h in heads]
        else:
            spm = sps
        cm = [_dot(_bf(spm[h]), suffix) for h in heads]
        ex = [zs[h] - sps[h] + (cm[h][:, :tk] + jnp.concatenate([carry[h]] * nrep, axis=1))
              for h in heads]
        av = [jnp.exp2(ex[h]) for h in heads]
        if masked:
            av = [jnp.where(mask, av[h], 0.0) for h in heads]
        pv = [_dot_nt(_bf(av[h]), vt) for h in heads]
        carry = tuple(carry[h] + cm[h][:, tk:] for h in heads)
        return carry, acc + jnp.where(first, pv[0], pv[1])

    zc = jnp.zeros((tq, LANE), F32)
    carry, acc = (zc, zc), zc
    nd = tq // tk
    j0 = i * nd
    for dj in reversed(range(nd)):
        carry, acc = tile(j0 + dj, carry, acc, True)

    def body(jj, st):
        carry, acc = tile(j0 - 1 - jj, (st[0], st[1]), st[2], False)
        return carry[0], carry[1], acc

    _, _, acc = lax.fori_loop(0, j0, body, (carry[0], carry[1], acc))
    o_ref[...] = _bf(acc)


def sb_attention_prompt(qb, kt16, vt16, sb_bias, nseq, seqlen, tq=SB_TQ):
    W = qb.shape[1]
    nt, tk = kt16.shape[1], kt16.shape[3]
    nq = seqlen // tq
    nhp = W // LANE
    kern = functools.partial(_sb_prompt_kernel, tq=tq, tk=tk)
    kv_spec = pl.BlockSpec((1, nt, LANE, tk), lambda b, hp, i: (b, 0, hp, 0))
    return pl.pallas_call(
        kern,
        grid=(nseq, nhp, nq),
        in_specs=[pl.BlockSpec(memory_space=pltpu.SMEM),
                  pl.BlockSpec((tq, LANE), lambda b, hp, i: (b * nq + i, hp)),
                  kv_spec, kv_spec],
        out_specs=pl.BlockSpec((tq, LANE), lambda b, hp, i: (b * nq + i, hp)),
        out_shape=jax.ShapeDtypeStruct((nseq * seqlen, W), BF16),
        compiler_params=_cparams(("parallel", "parallel", "arbitrary")),
        name="sb_attention_prompt",
    )(sb_bias, qb, kt16, vt16)


def _strict_lower_ones(n):
    r = lax.broadcasted_iota(jnp.int32, (n, n), 0)
    c = lax.broadcasted_iota(jnp.int32, (n, n), 1)
    return jnp.where(r > c, 1.0, 0.0).astype(BF16)


def _sb_sample_kernel(pt_ref, q_ref, ko_ref, vo_ref, bias_ref, *rest, n_pages, n_tok, page):
    k_refs = rest[:n_pages]
    v_refs = rest[n_pages:2 * n_pages]
    o_ref = rest[2 * n_pages]
    W = H_B * D_HB
    R = n_tok * H_B
    hrow = lax.broadcasted_iota(jnp.int32, (H_B, W), 0)
    hlane = lax.broadcasted_iota(jnp.int32, (H_B, W), 1) // D_HB
    hmask = hrow == hlane
    q = q_ref[...]
    qm = _bf(jnp.concatenate(
        [jnp.where(hmask, jnp.broadcast_to(q[t:t + 1, :], (H_B, W)), 0.0) for t in range(n_tok)],
        axis=0))
    bias2 = jnp.concatenate([bias_ref[...]] * n_tok, axis=0) * LOG2E
    scale2 = (D_HB ** -0.5) * LOG2E
    no = ko_ref.shape[0]
    trow = lax.broadcasted_iota(jnp.int32, (R, no), 0) // H_B
    scol = lax.broadcasted_iota(jnp.int32, (R, no), 1)
    own_mask = scol < trow
    order = list(reversed(range(n_pages)))
    zs = [_dot_nt(qm, _bf(ko_ref[...]))] + [_dot(qm, _bf(k_refs[p][0])) for p in order]
    zs = [z * scale2 + bias2 for z in zs]
    sps = [jnp.maximum(z, 0.0) + jnp.log(1.0 + jnp.exp2(-jnp.abs(z))) * LOG2E for z in zs]
    spb = [_bf(jnp.where(own_mask, sps[0], 0.0))] + [_bf(sp) for sp in sps[1:]]
    r = lax.broadcasted_iota(jnp.int32, (page, page + LANE), 0)
    c = lax.broadcasted_iota(jnp.int32, (page, page + LANE), 1)
    suffix = jnp.where(jnp.logical_or(c >= page, r > c), -1.0, 0.0).astype(BF16)
    own_local = _dot(spb[0], -_strict_lower_ones(no))
    own_total = _dot(spb[0], jnp.full((no, LANE), -1.0, BF16))
    cms = [_dot(sp, suffix) for sp in spb[1:]]
    carry = own_total
    exs = [zs[0] - sps[0] + own_local]
    for t, cm in enumerate(cms):
        exs.append(zs[t + 1] - sps[t + 1] + (cm[:, :page] + carry))
        carry = carry + cm[:, page:]
    avs = [jnp.exp2(ex) for ex in exs]
    acc = _dot(_bf(jnp.where(own_mask, avs[0], 0.0)), _bf(vo_ref[...]))
    for t, p in enumerate(order):
        acc = acc + _dot_nt(_bf(avs[t + 1]), _bf(v_refs[p][0]))
    rows = [jnp.sum(jnp.where(hmask, acc[t * H_B:(t + 1) * H_B, :], 0.0), axis=0, keepdims=True)
            for t in range(n_tok)]
    rows.append(jnp.zeros((q.shape[0] - n_tok, W), F32))
    o_ref[...] = jnp.concatenate(rows, axis=0)


def sb_attention_sample(qb, kb, vb, sb_bias_col, cache_k, cache_v, page_table_flat, layer,
                        row0, nseq, n_pages, n_tok):
    W = qb.shape[1]
    page = cache_k.shape[2]
    rb0 = row0 // SAMPLE_PAD
    own_q = pl.BlockSpec((SAMPLE_PAD, W), lambda b, pt: (rb0 + b, 0))
    own = pl.BlockSpec((SAMPLE_PAD, W), lambda b, pt: (b, 0))

    def page_spec(p):
        return pl.BlockSpec((1, W, page), lambda b, pt: (layer + pt[b * n_pages + p], 0, 0))

    kern = functools.partial(_sb_sample_kernel, n_pages=n_pages, n_tok=n_tok, page=page)
    grid_spec = pltpu.PrefetchScalarGridSpec(
        num_scalar_prefetch=1,
        grid=(nseq,),
        in_specs=[own_q, own, own, pl.BlockSpec((H_B, 1), lambda b, pt: (0, 0))]
                 + [page_spec(p) for p in range(n_pages)] * 2,
        out_specs=pl.BlockSpec((SAMPLE_PAD, W), lambda b, pt: (b, 0)),
    )
    return pl.pallas_call(
        kern,
        grid_spec=grid_spec,
        out_shape=jax.ShapeDtypeStruct((nseq * SAMPLE_PAD, W), F32),
        compiler_params=_cparams(("arbitrary",)),
        name="sb_attention_sample",
    )(page_table_flat, qb, kb, vb, sb_bias_col, *([cache_k] * n_pages), *([cache_v] * n_pages))


def _merge_kernel(x_ref, oa_ref, ob_ref, gla_ref, glb_ref, wa_ref, wb_ref, wo_ref, ln_ref,
                  o_ref, *, alpha):
    pa = _dot(oa_ref[...], wa_ref[...])
    pb = _dot(ob_ref[...], wb_ref[...])
    m = _sigmoid(gla_ref[...]) * pa + _sigmoid(glb_ref[...]) * pb
    y = _dot(_bf(m), wo_ref[...])
    ln = ln_ref[...]
    o_ref[...] = _layer_norm(alpha * x_ref[...] + y, ln[0:1, :], ln[1:2, :])


def merge_out_ln(x, oa, ob, gla, glb, wa, wb, wo, ln, alpha, tm=256):
    T, D = x.shape
    W = oa.shape[1]
    row = lambda i: (i, 0)
    full = lambda i: (0, 0)
    return pl.pallas_call(
        functools.partial(_merge_kernel, alpha=alpha),
        grid=(T // tm,),
        in_specs=[pl.BlockSpec((tm, D), row), pl.BlockSpec((tm, W), row), pl.BlockSpec((tm, W), row),
                  pl.BlockSpec((tm, D), row), pl.BlockSpec((tm, D), row),
                  pl.BlockSpec((W, D), full), pl.BlockSpec((W, D), full), pl.BlockSpec((D, D), full),
                  pl.BlockSpec((2, D), full)],
        out_specs=pl.BlockSpec((tm, D), row),
        out_shape=jax.ShapeDtypeStruct((T, D), F32),
        compiler_params=_cparams(("parallel",)),
        name="merge_out_ln",
    )(x, oa, ob, gla, glb, wa, wb, wo, ln)


def _router_kernel(x_ref, w_ref, g_ref, *, n_exp):
    logits = _dot3(x_ref[...], w_ref[...])
    lane = lax.broadcasted_iota(jnp.int32, logits.shape, 1)
    neg = jnp.float32(-jnp.inf)
    l1 = jnp.where(lane < n_exp, logits, neg)
    m1 = jnp.max(l1, axis=-1, keepdims=True)
    i1 = jnp.min(jnp.where(l1 == m1, lane, LANE), axis=-1, keepdims=True)
    l2 = jnp.where(lane == i1, neg, l1)
    m2 = jnp.max(l2, axis=-1, keepdims=True)
    i2 = jnp.min(jnp.where(l2 == m2, lane, LANE), axis=-1, keepdims=True)
    e = jnp.exp(m2 - m1)
    w1 = 1.0 / (1.0 + e)
    g_ref[...] = jnp.where(lane == i1, w1, 0.0) + jnp.where(lane == i2, e * w1, 0.0)


def router_gate(x, router_w, tm=512):
    T, D = x.shape
    n_exp = router_w.shape[1]
    wpad = jnp.pad(router_w, ((0, 0), (0, LANE - n_exp)))
    return pl.pallas_call(
        functools.partial(_router_kernel, n_exp=n_exp),
        grid=(T // tm,),
        in_specs=[pl.BlockSpec((tm, D), lambda i: (i, 0)), pl.BlockSpec((D, LANE), lambda i: (0, 0))],
        out_specs=pl.BlockSpec((tm, LANE), lambda i: (i, 0)),
        out_shape=jax.ShapeDtypeStruct((T, LANE), F32),
        compiler_params=_cparams(("parallel",)),
        name="router_gate",
    )(x, wpad)


def _ffn_kernel(x_ref, wg_ref, wu_ref, wd_ref, ln_ref, o_ref, acc_ref, *, alpha):
    f = pl.program_id(1)

    @pl.when(f == 0)
    def _():
        acc_ref[...] = jnp.zeros_like(acc_ref)

    x = x_ref[...]
    xb = _bf(x)
    h = _silu(_dot(xb, wg_ref[0])) * _dot(xb, wu_ref[0])
    acc_ref[...] += _dot(_bf(h), wd_ref[0])

    @pl.when(f == pl.num_programs(1) - 1)
    def _():
        ln = ln_ref[...]
        o_ref[...] = _layer_norm(alpha * x + acc_ref[...], ln[0:1, :], ln[1:2, :])


def ffn_ln(x, wg, wu, wd, j, ln, alpha, tm=512, tf=256):
    T, D = x.shape
    F = wg.shape[2]
    row = lambda i, f: (i, 0)
    return pl.pallas_call(
        functools.partial(_ffn_kernel, alpha=alpha),
        grid=(T // tm, F // tf),
        in_specs=[pl.BlockSpec((tm, D), row),
                  pl.BlockSpec((1, D, tf), lambda i, f: (j, 0, f)),
                  pl.BlockSpec((1, D, tf), lambda i, f: (j, 0, f)),
                  pl.BlockSpec((1, tf, D), lambda i, f: (j, f, 0)),
                  pl.BlockSpec((2, D), lambda i, f: (0, 0))],
        out_specs=pl.BlockSpec((tm, D), row),
        out_shape=jax.ShapeDtypeStruct((T, D), F32),
        scratch_shapes=[pltpu.VMEM((tm, D), F32)],
        compiler_params=_cparams(("parallel", "arbitrary")),
        name="ffn_ln",
    )(x, wg, wu, wd, ln)


MOE_TOKENS = 1024
MOE_ROWS = 256


def _moe_kernel(cnt_ref, x_ref, rkt_ref, rk_ref, gate_ref, wg_ref, wu_ref, wd_ref, ln_ref, o_ref,
                xb_ref, xs_ref, ya_ref, acc_ref, *, alpha, n_exp):
    i = pl.program_id(0)
    e = pl.program_id(1)
    f = pl.program_id(2)
    nf = pl.num_programs(2)
    TB, D = x_ref.shape
    cnt = cnt_ref[i * n_exp + e]
    nsub = (cnt + MOE_ROWS - 1) // MOE_ROWS

    @pl.when(jnp.logical_and(e == 0, f == 0))
    def _():
        acc_ref[...] = jnp.zeros_like(acc_ref)
        xb_ref[...] = _bf(x_ref[...])

    @pl.when(f == 0)
    def _():
        rke = rkt_ref[0, pl.ds(e, 1), :]
        rows = lax.broadcasted_iota(jnp.int32, (MOE_ROWS, TB), 0).astype(F32)

        def gather(s, carry):
            r0 = pl.multiple_of(s * MOE_ROWS, MOE_ROWS)
            onehot = jnp.where(rke - r0.astype(F32) == rows, 1.0, 0.0).astype(BF16)
            xs_ref[pl.ds(r0, MOE_ROWS), :] = _bf(_dot(onehot, xb_ref[...]))
            ya_ref[pl.ds(r0, MOE_ROWS), :] = jnp.zeros((MOE_ROWS, D), F32)
            return carry

        lax.fori_loop(0, nsub, gather, 0)

    def expert(s, carry):
        r0 = pl.multiple_of(s * MOE_ROWS, MOE_ROWS)
        xsub = xs_ref[pl.ds(r0, MOE_ROWS), :]
        h = _silu(_dot(xsub, wg_ref[0, 0])) * _dot(xsub, wu_ref[0, 0])
        ya_ref[pl.ds(r0, MOE_ROWS), :] += _dot(_bf(h), wd_ref[0, 0])
        return carry

    lax.fori_loop(0, nsub, expert, 0)

    @pl.when(f == nf - 1)
    def _():
        lane = lax.broadcasted_iota(jnp.int32, (TB, LANE), 1)
        ge = jnp.sum(jnp.where(lane == e, gate_ref[...], 0.0), axis=-1, keepdims=True)
        rkc = jnp.sum(jnp.where(lane == e, rk_ref[...], 0.0), axis=-1, keepdims=True)
        cols = lax.broadcasted_iota(jnp.int32, (TB, MOE_ROWS), 1).astype(F32)

        def combine(s, carry):
            r0 = pl.multiple_of(s * MOE_ROWS, MOE_ROWS)
            onehot_t = jnp.where(rkc - r0.astype(F32) == cols, 1.0, 0.0).astype(BF16)
            acc_ref[...] += ge * _dot(onehot_t, _bf(ya_ref[pl.ds(r0, MOE_ROWS), :]))
            return carry

        lax.fori_loop(0, nsub, combine, 0)

    @pl.when(jnp.logical_and(e == n_exp - 1, f == nf - 1))
    def _():
        ln = ln_ref[...]
        o_ref[...] = _layer_norm(alpha * x_ref[...] + acc_ref[...], ln[0:1, :], ln[1:2, :])


def moe_ln(x, gate, wg, wu, wd, j, ln, alpha, tf=512):
    T, D = x.shape
    _, E, _, F = wg.shape
    TB = MOE_TOKENS
    nb = T // TB
    selb = (gate[:, :E] > 0.0).reshape(nb, TB, E)
    rank = jnp.cumsum(selb.astype(jnp.int32), axis=1) - 1
    rank = jnp.where(selb, rank, -1).astype(F32)
    cnt = jnp.sum(selb.astype(jnp.int32), axis=1).reshape(-1)
    rk = jnp.pad(rank.reshape(T, E), ((0, 0), (0, LANE - E)), constant_values=-1.0)
    rkt = jnp.transpose(rank, (0, 2, 1))
    blk = lambda i, e, f, cnt: (i, 0)
    grid_spec = pltpu.PrefetchScalarGridSpec(
        num_scalar_prefetch=1,
        grid=(nb, E, F // tf),
        in_specs=[pl.BlockSpec((TB, D), blk),
                  pl.BlockSpec((1, E, TB), lambda i, e, f, cnt: (i, 0, 0)),
                  pl.BlockSpec((TB, LANE), blk), pl.BlockSpec((TB, LANE), blk),
                  pl.BlockSpec((1, 1, D, tf), lambda i, e, f, cnt: (j, e, 0, f)),
                  pl.BlockSpec((1, 1, D, tf), lambda i, e, f, cnt: (j, e, 0, f)),
                  pl.BlockSpec((1, 1, tf, D), lambda i, e, f, cnt: (j, e, f, 0)),
                  pl.BlockSpec((2, D), lambda i, e, f, cnt: (0, 0))],
        out_specs=pl.BlockSpec((TB, D), blk),
        scratch_shapes=[pltpu.VMEM((TB, D), BF16), pltpu.VMEM((TB, D), BF16),
                        pltpu.VMEM((TB, D), F32), pltpu.VMEM((TB, D), F32)],
    )
    return pl.pallas_call(
        functools.partial(_moe_kernel, alpha=alpha, n_exp=E),
        grid_spec=grid_spec,
        out_shape=jax.ShapeDtypeStruct((T, D), F32),
        compiler_params=_cparams(("parallel", "arbitrary", "arbitrary")),
        name="moe_ln",
    )(cnt, x, rkt, rk, gate, wg, wu, wd, ln)


def kernel(x_prompt, x_sample, cache_k, cache_v, state_delta, state_conv, page_table, w_in, conv_w,
           a_log, dt_bias, o_norm_w, sb_bias, w_proj_a, w_proj_b, w_out, ln1_g, ln1_b, ln2_g, ln2_b,
           ffn_w_gate, ffn_w_up, ffn_w_down, router_w, moe_w_gate, moe_w_up, moe_w_down):
    pb, seq, D = x_prompt.shape
    db, dseq, _ = x_sample.shape
    depth = w_in.shape[0]
    n_phys, page = cache_k.shape[1], cache_k.shape[2]
    n_pages = page_table.shape[1]
    W_b = H_B * D_HB
    C3 = 3 * H_A * DK_A
    alpha = (2.0 * depth) ** 0.25
    tp = pb * seq
    ts = db * SAMPLE_PAD

    xs_pad = jnp.pad(x_sample, ((0, 0), (0, SAMPLE_PAD - dseq), (0, 0)))
    x = jnp.concatenate([x_prompt.reshape(tp, D), xs_pad.reshape(ts, D)], axis=0)
    ck = jnp.transpose(cache_k, (0, 1, 3, 4, 2)).reshape(depth * n_phys, W_b, page)
    cv = jnp.transpose(cache_v, (0, 1, 3, 4, 2)).reshape(depth * n_phys, W_b, page)
    pt_flat = page_table.reshape(-1).astype(jnp.int32)
    buf0 = jnp.zeros((pb, CONV_W - 1, C3), F32)
    s00 = jnp.zeros((pb, H_A, DK_A, DV_A), F32)
    ffn_wg, ffn_wu, ffn_wd = _bf(ffn_w_gate), _bf(ffn_w_up), _bf(ffn_w_down)
    moe_wg, moe_wu, moe_wd = _bf(moe_w_gate), _bf(moe_w_up), _bf(moe_w_down)

    outs = {n: [] for n in ("kp", "vp", "sp", "cp", "ks", "vs", "ss", "cs")}
    for li in range(depth):
        w_main, wkv_t = pack_w_in(w_in[li])
        qkv, z, gla, glb, bd, qb = in_projection_main(x, w_main)
        kt32, vt32, kt16, vt16 = in_projection_kv(x, wkv_t, 0, tp, pb, seq, True)
        kb_s, vb_s = in_projection_kv(x, wkv_t, tp, ts, db, SAMPLE_PAD, False)
        head_par = jnp.zeros((2, LANE), F32)
        head_par = head_par.at[0, H_A:2 * H_A].set(-jnp.exp(a_log[li].astype(F32)))
        head_par = head_par.at[1, H_A:2 * H_A].set(dt_bias[li].astype(F32))
        norm_w = o_norm_w[li].reshape(1, DV_A).astype(F32)

        qp, kp_, vp_, cbuf_p = conv_prep(qkv, conv_w[li], buf0, 0, pb, seq, 512, seq)
        qs, ks_, vs_, cbuf_s = conv_prep(qkv, conv_w[li], state_conv[li], tp, db, SAMPLE_PAD,
                                         SAMPLE_PAD, dseq, G=16)
        oa_p, s_p = delta_rule_prompt(qp, kp_, vp_, bd, z, head_par, norm_w, s00, pb, seq,
                                      DELTA_CHUNK)
        oa_s, s_s = delta_rule_short(qs, ks_, vs_, bd, z, head_par, norm_w, state_delta[li],
                                     tp, db, SAMPLE_PAD, dseq)
        ob_p = sb_attention_prompt(qb, kt16, vt16, sb_bias[li].astype(F32), pb, seq)
        ob_s = sb_attention_sample(qb, kb_s, vb_s, sb_bias[li].astype(F32).reshape(H_B, 1), ck, cv,
                                   pt_flat, li * n_phys, tp, db, n_pages, dseq)
        oa = jnp.concatenate([oa_p, oa_s], axis=0)
        ob = jnp.concatenate([ob_p, _bf(ob_s)], axis=0)
        ln1 = jnp.stack([ln1_g[li], ln1_b[li]]).astype(F32)
        ln2 = jnp.stack([ln2_g[li], ln2_b[li]]).astype(F32)
        x1 = merge_out_ln(x, oa, ob, gla, glb, _bf(w_proj_a[li]), _bf(w_proj_b[li]), _bf(w_out[li]),
                          ln1, alpha)
        j = li // 2
        if li % 2 == 0:
            x = ffn_ln(x1, ffn_wg, ffn_wu, ffn_wd, j, ln2, alpha)
        else:
            gate = router_gate(x1, router_w[j].astype(F32))
            x = moe_ln(x1, gate, moe_wg, moe_wu, moe_wd, j, ln2, alpha)

        outs["kp"].append(jnp.transpose(kt32.reshape(pb, H_B, D_HB, seq), (0, 3, 1, 2)))
        outs["vp"].append(jnp.transpose(vt32.reshape(pb, H_B, D_HB, seq), (0, 3, 1, 2)))
        outs["sp"].append(s_p)
        outs["cp"].append(cbuf_p)
        outs["ks"].append(kb_s.reshape(db, SAMPLE_PAD, H_B, D_HB)[:, :dseq])
        outs["vs"].append(vb_s.reshape(db, SAMPLE_PAD, H_B, D_HB)[:, :dseq])
        outs["ss"].append(s_s)
        outs["cs"].append(cbuf_s)

    y_prompt = x[:tp].reshape(pb, seq, D)
    y_sample = x[tp:].reshape(db, SAMPLE_PAD, D)[:, :dseq]
    st = {n: jnp.stack(v) for n, v in outs.items()}
    return (y_prompt, y_sample, st["kp"], st["vp"], st["sp"], st["cp"],
            st["ks"], st["vs"], st["ss"], st["cs"])
```

```python
import functools
import math

import jax
import jax.numpy as jnp
from jax import lax
from jax.experimental import pallas as pl
from jax.experimental.pallas import tpu as pltpu

F32 = jnp.float32
BF16 = jnp.bfloat16

H_A = 8
DK_A = 64
DV_A = 64
H_B = 8
D_HB = 64
CONV_W = 4
DELTA_CHUNK = 64
LN_EPS = 1e-5
NORM_EPS = 1e-6
SAMPLE_PAD = 8
LANE = 128
VMEM_LIMIT = 56 * 1024 * 1024
LOG2E = 1.4426950408889634
SB_TQ = 512
SB_TK = 256


def _bf(x):
    return x.astype(BF16)


def _dot(a, b):
    return jnp.dot(a, b, preferred_element_type=F32)


def _dot_nt(a, b):
    return lax.dot_general(a, b, (((1,), (1,)), ((), ())), preferred_element_type=F32)


def _dot_tn(a, b):
    return lax.dot_general(a, b, (((0,), (0,)), ((), ())), preferred_element_type=F32)


def _split2(x):
    hi = _bf(x)
    lo = _bf(x - hi.astype(F32))
    return hi, lo


def _split3(x):
    hi = _bf(x)
    r = x - hi.astype(F32)
    mid = _bf(r)
    lo = _bf(r - mid.astype(F32))
    return hi, mid, lo


def _dot3s(a, b):
    return _dot(a[0], b[0]) + (_dot(a[0], b[1]) + _dot(a[1], b[0]))


def _dot3(a, b):
    return _dot3s(_split2(a), _split2(b))


def _silu(x):
    return x * (1.0 / (1.0 + jnp.exp(-x)))


def _sigmoid(x):
    return 1.0 / (1.0 + jnp.exp(-x))


def _softplus(x):
    return jnp.maximum(x, 0.0) + jnp.log1p(jnp.exp(-jnp.abs(x)))


def _layer_norm(x, g, b):
    mu = jnp.mean(x, axis=-1, keepdims=True)
    xc = x - mu
    var = jnp.mean(xc * xc, axis=-1, keepdims=True)
    return xc * lax.rsqrt(var + LN_EPS) * g + b


def _cparams(sem, vmem=VMEM_LIMIT):
    return pltpu.CompilerParams(dimension_semantics=sem, vmem_limit_bytes=vmem)


_PROJ_GROUPS = (("qkv", 1536, F32), ("z", 512, F32), ("gla", 1024, F32), ("glb", 1024, F32),
                ("bd", 128, F32), ("qb", 512, F32))
_PROJ_MAIN = sum(w for _, w, _ in _PROJ_GROUPS)
_W_KV = 2 * H_B * D_HB


def _proj_main_kernel(x_ref, w_ref, *o_refs):
    x = _bf(x_ref[...])
    off = 0
    for (_, width, dt), o_ref in zip(_PROJ_GROUPS, o_refs):
        o_ref[...] = _dot(x, w_ref[:, off:off + width]).astype(dt)
        off += width


def in_projection_main(x, w_main, tm=256):
    T, D = x.shape
    row = lambda i: (i, 0)
    return pl.pallas_call(
        _proj_main_kernel,
        grid=(T // tm,),
        in_specs=[pl.BlockSpec((tm, D), row), pl.BlockSpec((D, _PROJ_MAIN), lambda i: (0, 0))],
        out_specs=[pl.BlockSpec((tm, w), row) for _, w, _ in _PROJ_GROUPS],
        out_shape=[jax.ShapeDtypeStruct((T, w), dt) for _, w, dt in _PROJ_GROUPS],
        compiler_params=_cparams(("parallel",)),
        name="in_projection_main",
    )(x, w_main)


def _proj_kv_kernel(x_ref, wkv_ref, *o_refs, kv_transposed):
    x = _bf(x_ref[...])
    wb = H_B * D_HB
    if kv_transposed:
        k32, v32, kb16, vb16 = o_refs
        kt = _dot_nt(wkv_ref[:wb, :], x)
        vt = _dot_nt(wkv_ref[wb:, :], x)
        k32[0] = kt
        v32[0] = vt
        kb16[0, 0] = _bf(kt)
        vb16[0, 0] = _bf(vt)
    else:
        k32, v32 = o_refs
        k32[...] = _dot_nt(x, wkv_ref[:wb, :])
        v32[...] = _dot_nt(x, wkv_ref[wb:, :])


def in_projection_kv(x, wkv_t, row0, nrows, nseq, seqlen, kv_transposed, tm=SB_TK):
    D = x.shape[1]
    W = H_B * D_HB
    rb0 = row0 // tm
    row = lambda i: (i, 0)
    if kv_transposed:
        nt = seqlen // tm
        t32 = pl.BlockSpec((1, W, tm), lambda i: (i // nt, 0, i % nt))
        t16 = pl.BlockSpec((1, 1, W, tm), lambda i: (i // nt, i % nt, 0, 0))
        outs = [jax.ShapeDtypeStruct((nseq, W, seqlen), F32)] * 2
        outs += [jax.ShapeDtypeStruct((nseq, nt, W, tm), BF16)] * 2
        out_specs = [t32, t32, t16, t16]
    else:
        outs = [jax.ShapeDtypeStruct((nrows, W), F32)] * 2
        out_specs = [pl.BlockSpec((tm, W), row)] * 2
    return pl.pallas_call(
        functools.partial(_proj_kv_kernel, kv_transposed=kv_transposed),
        grid=(nrows // tm,),
        in_specs=[pl.BlockSpec((tm, D), lambda i: (rb0 + i, 0)),
                  pl.BlockSpec((_W_KV, D), lambda i: (0, 0))],
        out_specs=out_specs,
        out_shape=outs,
        compiler_params=_cparams(("parallel",)),
        name="in_projection_kv",
    )(x, wkv_t)


def pack_w_in(w_in):
    c = 3 * H_A * DK_A
    wv = H_A * DV_A
    wb = H_B * D_HB
    D = w_in.shape[0]
    i = 0
    qkv = w_in[:, i:i + c]; i += c
    z = w_in[:, i:i + wv]; i += wv
    bd = w_in[:, i:i + 2 * H_A]; i += 2 * H_A
    qb = w_in[:, i:i + wb]; i += wb
    kv = w_in[:, i:i + 2 * wb]; i += 2 * wb
    gla = w_in[:, i:i + D]; i += D
    glb = w_in[:, i:i + D]; i += D
    bd = jnp.pad(bd, ((0, 0), (0, LANE - 2 * H_A)))
    main = _bf(jnp.concatenate([qkv, z, gla, glb, bd, qb], axis=1))
    return main, _bf(kv.T)


def _prep_kernel(x_ref, w_ref, buf_ref, hsum_ref, q_ref, k_ref, v_ref, nb_ref, xp_ref,
                 *, tt, valid_last, G):
    t = pl.program_id(1)
    nt = pl.num_programs(1)
    halo = CONV_W - 1
    span = tt + 8
    w = w_ref[...]
    ys = []
    for g in range(G):
        base = g * span + 8

        @pl.when(t == 0)
        def _():
            xp_ref[base - halo:base, :] = buf_ref[g]

        if G == 1:
            @pl.when(t > 0)
            def _():
                xp_ref[base - halo:base, :] = xp_ref[base + tt - halo:base + tt, :]

        xp_ref[base:base + tt, :] = x_ref[g * tt:(g + 1) * tt, :]
        y = xp_ref[base - halo:base - halo + tt, :] * w[0:1, :]
        for i in range(1, CONV_W):
            y = y + xp_ref[base - halo + i:base - halo + i + tt, :] * w[i:i + 1, :]
        ys.append(y)
    y = _silu(ys[0] if G == 1 else jnp.concatenate(ys, axis=0))
    wq = H_A * DK_A
    q = y[:, :wq]
    k = y[:, wq:2 * wq]
    v_ref[...] = y[:, 2 * wq:]
    hs = hsum_ref[...]

    def l2n(a):
        hi, lo = _split2(a * a)
        ss = _dot(hi, hs) + _dot(lo, hs)
        return a * lax.rsqrt(ss + NORM_EPS)

    q_ref[...] = l2n(q) * (DK_A ** -0.5)
    k_ref[...] = l2n(k)

    @pl.when(t == nt - 1)
    def _():
        for g in range(G):
            base = g * span + 8
            nb_ref[g] = xp_ref[base + valid_last - halo:base + valid_last, :]


def conv_prep(qkv, conv_w, buf, row0, nseq, seqlen, tt, valid_len, G=1):
    C = qkv.shape[1]
    T = nseq * seqlen
    W = C // 3
    nt = seqlen // tt
    assert G == 1 or nt == 1
    R = G * tt
    rb0 = row0 // R
    valid_last = valid_len - (nt - 1) * tt
    head = jnp.arange(W, dtype=jnp.int32) // DK_A
    hsum = (head[:, None] == head[None, :]).astype(BF16)
    row_map = lambda b, t: (b * nt + t, 0)
    kern = functools.partial(_prep_kernel, tt=tt, valid_last=valid_last, G=G)
    return pl.pallas_call(
        kern,
        grid=(nseq // G, nt),
        in_specs=[pl.BlockSpec((R, C), lambda b, t: (rb0 + b * nt + t, 0)),
                  pl.BlockSpec((CONV_W, C), lambda b, t: (0, 0)),
                  pl.BlockSpec((G, CONV_W - 1, C), lambda b, t: (b, 0, 0)),
                  pl.BlockSpec((W, W), lambda b, t: (0, 0))],
        out_specs=[pl.BlockSpec((R, W), row_map),
                   pl.BlockSpec((R, W), row_map),
                   pl.BlockSpec((R, W), row_map),
                   pl.BlockSpec((G, CONV_W - 1, C), lambda b, t: (b, 0, 0))],
        out_shape=[jax.ShapeDtypeStruct((T, W), F32)] * 3
                  + [jax.ShapeDtypeStruct((nseq, CONV_W - 1, C), F32)],
        scratch_shapes=[pltpu.VMEM((G * (tt + 8), C), F32)],
        compiler_params=_cparams(("parallel", "arbitrary")),
        name="conv_prep",
    )(qkv, conv_w, buf, hsum)


def _delta_gates(bd, hp, C, valid_len, tri_incl):
    beta_all = _sigmoid(bd)
    g_all = hp[0:1, :] * _softplus(bd + hp[1:2, :])
    if valid_len < C:
        pos = lax.broadcasted_iota(jnp.int32, bd.shape, 0) % C
        live = pos < valid_len
        beta_all = jnp.where(live, beta_all, 0.0)
        g_all = jnp.where(live, g_all, 0.0)
    gcs = []
    for g in range(bd.shape[0] // C):
        gh, gm, gl = _split3(g_all[g * C:(g + 1) * C, :])
        gcs.append(_dot(tri_incl, gh) + (_dot(tri_incl, gm) + _dot(tri_incl, gl)))
    return beta_all, gcs


def _delta_intra(qs, ks, vs, betas, gcs, C, nstage):
    n = len(qs)
    row = lax.broadcasted_iota(jnp.int32, (C, C), 0)
    col = lax.broadcasted_iota(jnp.int32, (C, C), 1)
    incl = row >= col
    strict = row > col
    eye = row == col
    ones_cc = jnp.ones((C, C), BF16)
    rng = range(n)
    a_mats = [jnp.broadcast_to(gcs[i], (C, C)) for i in rng]
    dsp = [_split3(jnp.where(eye, a_mats[i], 0.0)) for i in rng]
    b_mats = [_dot_nt(ones_cc, dsp[i][0]) + (_dot_nt(ones_cc, dsp[i][1]) + _dot_nt(ones_cc, dsp[i][2]))
              for i in rng]
    decays = [jnp.where(incl, jnp.exp(jnp.minimum(a_mats[i] - b_mats[i], 0.0)), 0.0) for i in rng]
    kbs = [ks[i] * betas[i] for i in rng]
    kks = [_dot_nt(_bf(kbs[i]), _bf(ks[i])) for i in rng]
    qks = [_dot_nt(_bf(qs[i]), _bf(ks[i])) for i in rng]
    egs = [jnp.exp(gcs[i]) for i in rng]
    ps = [_split2(-jnp.where(strict, kks[i] * decays[i], 0.0)) for i in rng]
    rs = [jnp.concatenate([vs[i] * betas[i], kbs[i] * egs[i]], axis=-1) for i in rng]
    rsp = [_split2(rs[i]) for i in rng]
    dr = [_dot3s(ps[i], rsp[i]) for i in rng]
    rs = [rs[i] + dr[i] for i in rng]
    pb = [ps[i][0] for i in rng]
    for _ in range(nstage - 1):
        pb = [_bf(_dot(pb[i], pb[i])) for i in rng]
        dr = [_dot(pb[i], _bf(rs[i])) for i in rng]
        rs = [rs[i] + dr[i] for i in rng]
    us = [rs[i][:, :DV_A] for i in rng]
    ws = [rs[i][:, DV_A:] for i in rng]
    qkm = [jnp.where(incl, qks[i] * decays[i], 0.0) for i in rng]
    kdecs = [ks[i] * jnp.exp(gcs[i][C - 1:C, :] - gcs[i]) for i in rng]
    qgs = [qs[i] * egs[i] for i in rng]
    return us, ws, qkm, kdecs, qgs


def _delta_state(us, ws, qkm, kdecs, qgs, egl, states):
    rng = range(len(us))
    sb = [_bf(states[i]) for i in rng]
    ws_s = [_dot(ws[i], sb[i]) for i in rng]
    qs_s = [_dot(qgs[i], sb[i]) for i in rng]
    vn = [_bf(us[i] - ws_s[i]) for i in rng]
    os_ = [qs_s[i] + _dot(qkm[i], vn[i]) for i in rng]
    new = [states[i] * egl[i] + _dot_tn(kdecs[i], vn[i]) for i in rng]
    return os_, new


def _gated_rms(o, nw, z):
    rn = o * lax.rsqrt(jnp.mean(o * o, axis=-1, keepdims=True) + NORM_EPS)
    return rn * nw * _silu(z)


def _delta_intra_kernel(q_ref, k_ref, v_ref, bd_ref, hp_ref,
                        u_ref, w_ref, qk_ref, kd_ref, qg_ref, egl_ref, *, C, G, nstage):
    row = lax.broadcasted_iota(jnp.int32, (C, C), 0)
    col = lax.broadcasted_iota(jnp.int32, (C, C), 1)
    tri_incl = jnp.where(row >= col, 1.0, 0.0).astype(BF16)
    beta_all, gcs = _delta_gates(bd_ref[...], hp_ref[...], C, C, tri_incl)
    qs, ks, vs, betas, gch = [], [], [], [], []
    for g in range(G):
        rows = slice(g * C, (g + 1) * C)
        for h in range(H_A):
            sl = slice(h * DK_A, (h + 1) * DK_A)
            qs.append(q_ref[rows, sl])
            ks.append(k_ref[rows, sl])
            vs.append(v_ref[rows, sl])
            betas.append(beta_all[rows, h:h + 1])
            gch.append(gcs[g][:, H_A + h:H_A + h + 1])
    us, ws, qkm, kdecs, qgs = _delta_intra(qs, ks, vs, betas, gch, C, nstage)
    for g in range(G):
        rows = slice(g * C, (g + 1) * C)
        pr = slice(g * H_A, (g + 1) * H_A)
        u_ref[rows, :] = jnp.concatenate(us[pr], axis=-1)
        w_ref[rows, :] = _bf(jnp.concatenate(ws[pr], axis=-1))
        qk_ref[rows, :] = _bf(jnp.concatenate(qkm[pr], axis=-1))
        kd_ref[rows, :] = _bf(jnp.concatenate(kdecs[pr], axis=-1))
        qg_ref[rows, :] = _bf(jnp.concatenate(qgs[pr], axis=-1))
        egl_ref[0, g] = jnp.exp(gcs[g][C - 1:C, :])


def _delta_seq_kernel(u_ref, w_ref, qk_ref, kd_ref, qg_ref, egl_ref, nw_ref, s0_ref, *rest, nb):
    z_refs = rest[:nb]
    o_ref, sf_ref, s_scr = rest[nb:]
    c = pl.program_id(0)
    nc = pl.num_programs(0)

    @pl.when(c == 0)
    def _():
        s_scr[...] = s0_ref[...]

    nw = nw_ref[...]
    us, ws, qkm, kdecs, qgs, egl, states = [], [], [], [], [], [], []
    for b in range(nb):
        er = egl_ref[b, 0]
        for h in range(H_A):
            sl = slice(h * DK_A, (h + 1) * DK_A)
            us.append(u_ref[b, :, sl])
            ws.append(w_ref[b, :, sl])
            qkm.append(qk_ref[b, :, sl])
            kdecs.append(kd_ref[b, :, sl])
            qgs.append(qg_ref[b, :, sl])
            egl.append(er[:, H_A + h:H_A + h + 1])
            states.append(s_scr[b, h])
    os_, new = _delta_state(us, ws, qkm, kdecs, qgs, egl, states)
    for b in range(nb):
        outs = []
        for h in range(H_A):
            i = b * H_A + h
            s_scr[b, h] = new[i]
            outs.append(_gated_rms(os_[i], nw, z_refs[b][:, h * DV_A:(h + 1) * DV_A]))
        o_ref[b] = _bf(jnp.concatenate(outs, axis=-1))

    @pl.when(c == nc - 1)
    def _():
        sf_ref[...] = s_scr[...]


def delta_rule_prompt(q, k, v, bd, z, head_par, norm_w, s0, nseq, seqlen, C, G=4):
    T, W = q.shape
    nc = seqlen // C
    nstage = max(1, int(math.log2(C)))
    R = G * C
    ncb = nc // G
    row = lambda i: (i, 0)
    intra = pl.pallas_call(
        functools.partial(_delta_intra_kernel, C=C, G=G, nstage=nstage),
        grid=(T // R,),
        in_specs=[pl.BlockSpec((R, W), row)] * 3
                 + [pl.BlockSpec((R, LANE), row), pl.BlockSpec((2, LANE), lambda i: (0, 0))],
        out_specs=[pl.BlockSpec((R, W), row)] * 5
                  + [pl.BlockSpec((1, G, 1, LANE), lambda i: (i // ncb, i % ncb, 0, 0))],
        out_shape=[jax.ShapeDtypeStruct((T, W), F32)] + [jax.ShapeDtypeStruct((T, W), BF16)] * 4
                  + [jax.ShapeDtypeStruct((nseq, nc, 1, LANE), F32)],
        compiler_params=_cparams(("parallel",)),
        name="delta_intra",
    )
    u, w, qk, kd, qg, egl = intra(q, k, v, bd, head_par)
    r3 = lambda a: a.reshape(nseq, seqlen, W)
    blk = pl.BlockSpec((nseq, C, W), lambda c: (0, c, 0))
    st = pl.BlockSpec((nseq, H_A, DK_A, DV_A), lambda c: (0, 0, 0, 0))
    o, sf = pl.pallas_call(
        functools.partial(_delta_seq_kernel, nb=nseq),
        grid=(nc,),
        in_specs=[blk] * 5 + [pl.BlockSpec((nseq, 1, 1, LANE), lambda c: (0, c, 0, 0)),
                              pl.BlockSpec((1, DV_A), lambda c: (0, 0)), st]
                 + [pl.BlockSpec((C, W), functools.partial(lambda c, b: (b * nc + c, 0), b=b))
                    for b in range(nseq)],
        out_specs=[blk, st],
        out_shape=[jax.ShapeDtypeStruct((nseq, seqlen, W), BF16),
                   jax.ShapeDtypeStruct((nseq, H_A, DK_A, DV_A), F32)],
        scratch_shapes=[pltpu.VMEM((nseq, H_A, DK_A, DV_A), F32)],
        compiler_params=_cparams(("arbitrary",)),
        name="delta_seq",
    )(r3(u), r3(w), r3(qk), r3(kd), r3(qg), egl, norm_w, s0, *([z] * nseq))
    return o.reshape(T, W), sf


def _delta_short_kernel(q_ref, k_ref, v_ref, bd_ref, z_ref, hp_ref, nw_ref, s0_ref,
                        o_ref, sf_ref, *, C, G, valid_len, nstage):
    row = lax.broadcasted_iota(jnp.int32, (C, C), 0)
    col = lax.broadcasted_iota(jnp.int32, (C, C), 1)
    tri_incl = jnp.where(row >= col, 1.0, 0.0).astype(BF16)
    beta_all, gcs = _delta_gates(bd_ref[...], hp_ref[...], C, valid_len, tri_incl)
    qs, ks, vs, betas, gch, egl, states = [], [], [], [], [], [], []
    for g in range(G):
        rows = slice(g * C, (g + 1) * C)
        for h in range(H_A):
            sl = slice(h * DK_A, (h + 1) * DK_A)
            qs.append(q_ref[rows, sl])
            ks.append(k_ref[rows, sl])
            vs.append(v_ref[rows, sl])
            betas.append(beta_all[rows, h:h + 1])
            gc = gcs[g][:, H_A + h:H_A + h + 1]
            gch.append(gc)
            egl.append(jnp.exp(gc[C - 1:C, :]))
            states.append(s0_ref[g, h])
    us, ws, qkm, kdecs, qgs = _delta_intra(qs, ks, vs, betas, gch, C, nstage)
    os_, new = _delta_state(us, [_bf(a) for a in ws], [_bf(a) for a in qkm],
                            [_bf(a) for a in kdecs], [_bf(a) for a in qgs], egl, states)
    nw = nw_ref[...]
    for g in range(G):
        rows = slice(g * C, (g + 1) * C)
        outs = []
        for h in range(H_A):
            i = g * H_A + h
            sf_ref[g, h] = new[i]
            outs.append(_gated_rms(os_[i], nw, z_ref[rows, h * DV_A:(h + 1) * DV_A]))
        o_ref[rows, :] = _bf(jnp.concatenate(outs, axis=-1))


def delta_rule_short(q, k, v, bd, z, head_par, norm_w, s0, row0, nseq, C, valid_len, G=8):
    T, W = q.shape
    nstage = max(1, int(math.log2(C)))
    R = G * C
    rb0 = row0 // R
    row = lambda i: (i, 0)
    off = lambda i: (rb0 + i, 0)
    st = pl.BlockSpec((G, H_A, DK_A, DV_A), lambda i: (i, 0, 0, 0))
    return pl.pallas_call(
        functools.partial(_delta_short_kernel, C=C, G=G, valid_len=valid_len, nstage=nstage),
        grid=(nseq // G,),
        in_specs=[pl.BlockSpec((R, W), row)] * 3
                 + [pl.BlockSpec((R, LANE), off), pl.BlockSpec((R, W), off),
                    pl.BlockSpec((2, LANE), lambda i: (0, 0)),
                    pl.BlockSpec((1, DV_A), lambda i: (0, 0)), st],
        out_specs=[pl.BlockSpec((R, W), row), st],
        out_shape=[jax.ShapeDtypeStruct((T, W), BF16),
                   jax.ShapeDtypeStruct((nseq, H_A, DK_A, DV_A), F32)],
        compiler_params=_cparams(("parallel",)),
        name="delta_short",
    )(q, k, v, bd, z, head_par, norm_w, s0)


def _softplus2(z2):
    return jnp.maximum(jnp.log(1.0 + jnp.exp2(jnp.minimum(z2, 64.0))) * LOG2E, z2)


def _sb_prompt_kernel(bias_ref, q_ref, kt_ref, vt_ref, o_ref, *, tq, tk):
    hp = pl.program_id(1)
    i = pl.program_id(2)
    lane = lax.broadcasted_iota(jnp.int32, (tq, LANE), 1)
    first = lane < D_HB
    q = _bf(q_ref[...])
    zero = jnp.zeros_like(q)
    qm = (jnp.where(first, q, zero), jnp.where(first, zero, q))
    scale2 = (D_HB ** -0.5) * LOG2E
    bias2 = (bias_ref[2 * hp] * LOG2E, bias_ref[2 * hp + 1] * LOG2E)
    r = lax.broadcasted_iota(jnp.int32, (tk, tk + LANE), 0)
    c = lax.broadcasted_iota(jnp.int32, (tk, tk + LANE), 1)
    suffix = jnp.where(jnp.logical_or(c >= tk, r > c), -1.0, 0.0).astype(BF16)
    nrep = tk // LANE
    heads = (0, 1)

    def tile(j, carry, acc, masked, row0=0):
        kt = kt_ref[0, j]
        vt = vt_ref[0, j]
        live = slice(row0, tq)
        rows = tq - row0
        zs = [_dot(qm[h][live], kt) * scale2 + bias2[h] for h in heads]
        sps = [_softplus2(zs[h]) for h in heads]
        if masked:
            qpos = i * tq + row0 + lax.broadcasted_iota(jnp.int32, (rows, tk), 0)
            kpos = j * tk + lax.broadcasted_iota(jnp.int32, (rows, tk), 1)
            mask = kpos < qpos
            spm = [jnp.where(mask, sps[h], 0.0) for h in heads]
        else:
            spm = sps
        cm = [_dot(_bf(spm[h]), suffix) for h in heads]
        ex = [zs[h] - sps[h] + (cm[h][:, :tk] + jnp.concatenate([carry[h][live]] * nrep, axis=1))
              for h in heads]
        av = [jnp.exp2(ex[h]) for h in heads]
        if masked:
            av = [jnp.where(mask, av[h], 0.0) for h in heads]
        pv = [_dot_nt(_bf(av[h]), vt) for h in heads]
        new_c = [carry[h][live] + cm[h][:, tk:] for h in heads]
        head0 = lax.broadcasted_iota(jnp.int32, (rows, LANE), 1) < D_HB
        new_a = acc[live] + jnp.where(head0, pv[0], pv[1])
        if row0:
            new_c = [jnp.concatenate([carry[h][:row0], new_c[h]], axis=0) for h in heads]
            new_a = jnp.concatenate([acc[:row0], new_a], axis=0)
        return tuple(new_c), new_a

    zc = jnp.zeros((tq, LANE), F32)
    carry, acc = (zc, zc), zc
    nd = tq // tk
    j0 = i * nd
    for dj in reversed(range(nd)):
        carry, acc = tile(j0 + dj, carry, acc, True, row0=dj * tk)

    def body(jj, st):
        carry, acc = tile(j0 - 1 - jj, (st[0], st[1]), st[2], False)
        return carry[0], carry[1], acc

    _, _, acc = lax.fori_loop(0, j0, body, (carry[0], carry[1], acc))
    o_ref[...] = _bf(acc)


def sb_attention_prompt(qb, kt16, vt16, sb_bias, nseq, seqlen, tq=SB_TQ):
    W = qb.shape[1]
    nt, tk = kt16.shape[1], kt16.shape[3]
    nq = seqlen // tq
    nhp = W // LANE
    kern = functools.partial(_sb_prompt_kernel, tq=tq, tk=tk)
    kv_spec = pl.BlockSpec((1, nt, LANE, tk), lambda b, hp, i: (b, 0, hp, 0))
    return pl.pallas_call(
        kern,
        grid=(nseq, nhp, nq),
        in_specs=[pl.BlockSpec(memory_space=pltpu.SMEM),
                  pl.BlockSpec((tq, LANE), lambda b, hp, i: (b * nq + i, hp)),
                  kv_spec, kv_spec],
        out_specs=pl.BlockSpec((tq, LANE), lambda b, hp, i: (b * nq + i, hp)),
        out_shape=jax.ShapeDtypeStruct((nseq * seqlen, W), BF16),
        compiler_params=_cparams(("parallel", "parallel", "arbitrary")),
        name="sb_attention_prompt",
    )(sb_bias, qb, kt16, vt16)


def _strict_lower_ones(n):
    r = lax.broadcasted_iota(jnp.int32, (n, n), 0)
    c = lax.broadcasted_iota(jnp.int32, (n, n), 1)
    return jnp.where(r > c, 1.0, 0.0).astype(BF16)


def _sb_sample_kernel(pt_ref, q_ref, ko_ref, vo_ref, bias_ref, *rest, n_pages, n_tok, page):
    k_refs = rest[:n_pages]
    v_refs = rest[n_pages:2 * n_pages]
    o_ref = rest[2 * n_pages]
    W = H_B * D_HB
    R = n_tok * H_B
    hrow = lax.broadcasted_iota(jnp.int32, (H_B, W), 0)
    hlane = lax.broadcasted_iota(jnp.int32, (H_B, W), 1) // D_HB
    hmask = hrow == hlane
    q = q_ref[...]
    qm = _bf(jnp.concatenate(
        [jnp.where(hmask, jnp.broadcast_to(q[t:t + 1, :], (H_B, W)), 0.0) for t in range(n_tok)],
        axis=0))
    bias2 = jnp.concatenate([bias_ref[...]] * n_tok, axis=0) * LOG2E
    scale2 = (D_HB ** -0.5) * LOG2E
    no = ko_ref.shape[0]
    trow = lax.broadcasted_iota(jnp.int32, (R, no), 0) // H_B
    scol = lax.broadcasted_iota(jnp.int32, (R, no), 1)
    own_mask = scol < trow
    order = list(reversed(range(n_pages)))
    zs = [_dot_nt(qm, _bf(ko_ref[...]))] + [_dot(qm, _bf(k_refs[p][0])) for p in order]
    zs = [z * scale2 + bias2 for z in zs]
    sps = [_softplus2(z) for z in zs]
    spb = [_bf(jnp.where(own_mask, sps[0], 0.0))] + [_bf(sp) for sp in sps[1:]]
    r = lax.broadcasted_iota(jnp.int32, (page, page + LANE), 0)
    c = lax.broadcasted_iota(jnp.int32, (page, page + LANE), 1)
    suffix = jnp.where(jnp.logical_or(c >= page, r > c), -1.0, 0.0).astype(BF16)
    own_local = _dot(spb[0], -_strict_lower_ones(no))
    own_total = _dot(spb[0], jnp.full((no, LANE), -1.0, BF16))
    cms = [_dot(sp, suffix) for sp in spb[1:]]
    carry = own_total
    exs = [zs[0] - sps[0] + own_local]
    for t, cm in enumerate(cms):
        exs.append(zs[t + 1] - sps[t + 1] + (cm[:, :page] + carry))
        carry = carry + cm[:, page:]
    avs = [jnp.exp2(ex) for ex in exs]
    acc = _dot(_bf(jnp.where(own_mask, avs[0], 0.0)), _bf(vo_ref[...]))
    for t, p in enumerate(order):
        acc = acc + _dot_nt(_bf(avs[t + 1]), _bf(v_refs[p][0]))
    rows = [jnp.sum(jnp.where(hmask, acc[t * H_B:(t + 1) * H_B, :], 0.0), axis=0, keepdims=True)
            for t in range(n_tok)]
    rows.append(jnp.zeros((q.shape[0] - n_tok, W), F32))
    o_ref[...] = jnp.concatenate(rows, axis=0)


def sb_attention_sample(qb, kb, vb, sb_bias_col, cache_k, cache_v, page_table_flat, layer,
                        row0, nseq, n_pages, n_tok):
    W = qb.shape[1]
    page = cache_k.shape[2]
    rb0 = row0 // SAMPLE_PAD
    own_q = pl.BlockSpec((SAMPLE_PAD, W), lambda b, pt: (rb0 + b, 0))
    own = pl.BlockSpec((SAMPLE_PAD, W), lambda b, pt: (b, 0))

    def page_spec(p):
        return pl.BlockSpec((1, W, page), lambda b, pt: (layer + pt[b * n_pages + p], 0, 0))

    kern = functools.partial(_sb_sample_kernel, n_pages=n_pages, n_tok=n_tok, page=page)
    grid_spec = pltpu.PrefetchScalarGridSpec(
        num_scalar_prefetch=1,
        grid=(nseq,),
        in_specs=[own_q, own, own, pl.BlockSpec((H_B, 1), lambda b, pt: (0, 0))]
                 + [page_spec(p) for p in range(n_pages)] * 2,
        out_specs=pl.BlockSpec((SAMPLE_PAD, W), lambda b, pt: (b, 0)),
    )
    return pl.pallas_call(
        kern,
        grid_spec=grid_spec,
        out_shape=jax.ShapeDtypeStruct((nseq * SAMPLE_PAD, W), F32),
        compiler_params=_cparams(("arbitrary",)),
        name="sb_attention_sample",
    )(page_table_flat, qb, kb, vb, sb_bias_col, *([cache_k] * n_pages), *([cache_v] * n_pages))


def _merge_kernel(x_ref, oa_ref, ob_ref, gla_ref, glb_ref, wa_ref, wb_ref, wo_ref, ln_ref,
                  o_ref, *, alpha):
    pa = _dot(oa_ref[...], wa_ref[...])
    pb = _dot(ob_ref[...], wb_ref[...])
    m = _sigmoid(gla_ref[...]) * pa + _sigmoid(glb_ref[...]) * pb
    y = _dot(_bf(m), wo_ref[...])
    ln = ln_ref[...]
    o_ref[...] = _layer_norm(alpha * x_ref[...] + y, ln[0:1, :], ln[1:2, :])


def merge_out_ln(x, oa, ob, gla, glb, wa, wb, wo, ln, alpha, tm=256):
    T, D = x.shape
    W = oa.shape[1]
    row = lambda i: (i, 0)
    full = lambda i: (0, 0)
    return pl.pallas_call(
        functools.partial(_merge_kernel, alpha=alpha),
        grid=(T // tm,),
        in_specs=[pl.BlockSpec((tm, D), row), pl.BlockSpec((tm, W), row), pl.BlockSpec((tm, W), row),
                  pl.BlockSpec((tm, D), row), pl.BlockSpec((tm, D), row),
                  pl.BlockSpec((W, D), full), pl.BlockSpec((W, D), full), pl.BlockSpec((D, D), full),
                  pl.BlockSpec((2, D), full)],
        out_specs=pl.BlockSpec((tm, D), row),
        out_shape=jax.ShapeDtypeStruct((T, D), F32),
        compiler_params=_cparams(("parallel",)),
        name="merge_out_ln",
    )(x, oa, ob, gla, glb, wa, wb, wo, ln)


def _router_kernel(x_ref, w_ref, g_ref, *, n_exp):
    logits = _dot3(x_ref[...], w_ref[...])
    lane = lax.broadcasted_iota(jnp.int32, logits.shape, 1)
    neg = jnp.float32(-jnp.inf)
    l1 = jnp.where(lane < n_exp, logits, neg)
    m1 = jnp.max(l1, axis=-1, keepdims=True)
    i1 = jnp.min(jnp.where(l1 == m1, lane, LANE), axis=-1, keepdims=True)
    l2 = jnp.where(lane == i1, neg, l1)
    m2 = jnp.max(l2, axis=-1, keepdims=True)
    i2 = jnp.min(jnp.where(l2 == m2, lane, LANE), axis=-1, keepdims=True)
    e = jnp.exp(m2 - m1)
    w1 = 1.0 / (1.0 + e)
    g_ref[...] = jnp.where(lane == i1, w1, 0.0) + jnp.where(lane == i2, e * w1, 0.0)


def router_gate(x, router_w, tm=512):
    T, D = x.shape
    n_exp = router_w.shape[1]
    wpad = jnp.pad(router_w, ((0, 0), (0, LANE - n_exp)))
    return pl.pallas_call(
        functools.partial(_router_kernel, n_exp=n_exp),
        grid=(T // tm,),
        in_specs=[pl.BlockSpec((tm, D), lambda i: (i, 0)), pl.BlockSpec((D, LANE), lambda i: (0, 0))],
        out_specs=pl.BlockSpec((tm, LANE), lambda i: (i, 0)),
        out_shape=jax.ShapeDtypeStruct((T, LANE), F32),
        compiler_params=_cparams(("parallel",)),
        name="router_gate",
    )(x, wpad)


def _ffn_kernel(x_ref, wg_ref, wu_ref, wd_ref, ln_ref, o_ref, acc_ref, *, alpha):
    f = pl.program_id(1)

    @pl.when(f == 0)
    def _():
        acc_ref[...] = jnp.zeros_like(acc_ref)

    x = x_ref[...]
    xb = _bf(x)
    h = _silu(_dot(xb, wg_ref[0])) * _dot(xb, wu_ref[0])
    acc_ref[...] += _dot(_bf(h), wd_ref[0])

    @pl.when(f == pl.num_programs(1) - 1)
    def _():
        ln = ln_ref[...]
        o_ref[...] = _layer_norm(alpha * x + acc_ref[...], ln[0:1, :], ln[1:2, :])


def ffn_ln(x, wg, wu, wd, j, ln, alpha, tm=512):
    T, D = x.shape
    F = wg.shape[2]
    tf = next(c for c in (1408, 1024, 512, 256, 128) if F % c == 0)
    row = lambda i, f: (i, 0)
    return pl.pallas_call(
        functools.partial(_ffn_kernel, alpha=alpha),
        grid=(T // tm, F // tf),
        in_specs=[pl.BlockSpec((tm, D), row),
                  pl.BlockSpec((1, D, tf), lambda i, f: (j, 0, f)),
                  pl.BlockSpec((1, D, tf), lambda i, f: (j, 0, f)),
                  pl.BlockSpec((1, tf, D), lambda i, f: (j, f, 0)),
                  pl.BlockSpec((2, D), lambda i, f: (0, 0))],
        out_specs=pl.BlockSpec((tm, D), row),
        out_shape=jax.ShapeDtypeStruct((T, D), F32),
        scratch_shapes=[pltpu.VMEM((tm, D), F32)],
        compiler_params=_cparams(("parallel", "arbitrary")),
        name="ffn_ln",
    )(x, wg, wu, wd, ln)


MOE_TOKENS = 1024
MOE_ROWS = 256
MOE_TAIL_ROWS = 128


def _moe_kernel(cnt_ref, x_ref, rkt_ref, rk_ref, gate_ref, wg_ref, wu_ref, wd_ref, ln_ref, o_ref,
                xb_ref, xs_ref, ya_ref, acc_ref, *, alpha, n_exp):
    i = pl.program_id(0)
    e = pl.program_id(1)
    f = pl.program_id(2)
    nf = pl.num_programs(2)
    TB, D = x_ref.shape
    cnt = cnt_ref[i * n_exp + e]
    n_full = cnt // MOE_ROWS
    rem = cnt - n_full * MOE_ROWS

    def for_tiles(fn):
        def body(s, carry):
            fn(pl.multiple_of(s * MOE_ROWS, MOE_ROWS), MOE_ROWS)
            return carry

        lax.fori_loop(0, n_full, body, 0)
        r_tail = pl.multiple_of(n_full * MOE_ROWS, MOE_ROWS)

        @pl.when(rem > MOE_TAIL_ROWS)
        def _():
            fn(r_tail, MOE_ROWS)

        @pl.when(jnp.logical_and(rem > 0, rem <= MOE_TAIL_ROWS))
        def _():
            fn(r_tail, MOE_TAIL_ROWS)

    @pl.when(jnp.logical_and(e == 0, f == 0))
    def _():
        acc_ref[...] = jnp.zeros_like(acc_ref)
        xb_ref[...] = _bf(x_ref[...])

    @pl.when(f == 0)
    def _():
        rke = rkt_ref[0, pl.ds(e, 1), :]

        def gather(r0, rows):
            ridx = lax.broadcasted_iota(jnp.int32, (rows, TB), 0).astype(F32)
            onehot = jnp.where(rke - r0.astype(F32) == ridx, 1.0, 0.0).astype(BF16)
            xs_ref[pl.ds(r0, rows), :] = _bf(_dot(onehot, xb_ref[...]))
            ya_ref[pl.ds(r0, rows), :] = jnp.zeros((rows, D), F32)

        for_tiles(gather)

    def expert(r0, rows):
        xsub = xs_ref[pl.ds(r0, rows), :]
        h = _silu(_dot(xsub, wg_ref[0, 0])) * _dot(xsub, wu_ref[0, 0])
        ya_ref[pl.ds(r0, rows), :] += _dot(_bf(h), wd_ref[0, 0])

    for_tiles(expert)

    @pl.when(f == nf - 1)
    def _():
        lane = lax.broadcasted_iota(jnp.int32, (TB, LANE), 1)
        ge = jnp.sum(jnp.where(lane == e, gate_ref[...], 0.0), axis=-1, keepdims=True)
        rkc = jnp.sum(jnp.where(lane == e, rk_ref[...], 0.0), axis=-1, keepdims=True)

        def combine(r0, rows):
            cidx = lax.broadcasted_iota(jnp.int32, (TB, rows), 1).astype(F32)
            onehot_t = jnp.where(rkc - r0.astype(F32) == cidx, 1.0, 0.0).astype(BF16)
            acc_ref[...] += ge * _dot(onehot_t, _bf(ya_ref[pl.ds(r0, rows), :]))

        for_tiles(combine)

    @pl.when(jnp.logical_and(e == n_exp - 1, f == nf - 1))
    def _():
        ln = ln_ref[...]
        o_ref[...] = _layer_norm(alpha * x_ref[...] + acc_ref[...], ln[0:1, :], ln[1:2, :])


def moe_ln(x, gate, wg, wu, wd, j, ln, alpha, tf=512):
    T, D = x.shape
    _, E, _, F = wg.shape
    TB = MOE_TOKENS
    nb = T // TB
    selb = (gate[:, :E] > 0.0).reshape(nb, TB, E)
    rank = jnp.cumsum(selb.astype(jnp.int32), axis=1) - 1
    rank = jnp.where(selb, rank, -1).astype(F32)
    cnt = jnp.sum(selb.astype(jnp.int32), axis=1).reshape(-1)
    rk = jnp.pad(rank.reshape(T, E), ((0, 0), (0, LANE - E)), constant_values=-1.0)
    rkt = jnp.transpose(rank, (0, 2, 1))
    blk = lambda i, e, f, cnt: (i, 0)
    grid_spec = pltpu.PrefetchScalarGridSpec(
        num_scalar_prefetch=1,
        grid=(nb, E, F // tf),
        in_specs=[pl.BlockSpec((TB, D), blk),
                  pl.BlockSpec((1, E, TB), lambda i, e, f, cnt: (i, 0, 0)),
                  pl.BlockSpec((TB, LANE), blk), pl.BlockSpec((TB, LANE), blk),
                  pl.BlockSpec((1, 1, D, tf), lambda i, e, f, cnt: (j, e, 0, f)),
                  pl.BlockSpec((1, 1, D, tf), lambda i, e, f, cnt: (j, e, 0, f)),
                  pl.BlockSpec((1, 1, tf, D), lambda i, e, f, cnt: (j, e, f, 0)),
                  pl.BlockSpec((2, D), lambda i, e, f, cnt: (0, 0))],
        out_specs=pl.BlockSpec((TB, D), blk),
        scratch_shapes=[pltpu.VMEM((TB, D), BF16), pltpu.VMEM((TB, D), BF16),
                        pltpu.VMEM((TB, D), F32), pltpu.VMEM((TB, D), F32)],
    )
    return pl.pallas_call(
        functools.partial(_moe_kernel, alpha=alpha, n_exp=E),
        grid_spec=grid_spec,
        out_shape=jax.ShapeDtypeStruct((T, D), F32),
        compiler_params=_cparams(("parallel", "arbitrary", "arbitrary")),
        name="moe_ln",
    )(cnt, x, rkt, rk, gate, wg, wu, wd, ln)


def kernel(x_prompt, x_sample, cache_k, cache_v, state_delta, state_conv, page_table, w_in, conv_w,
           a_log, dt_bias, o_norm_w, sb_bias, w_proj_a, w_proj_b, w_out, ln1_g, ln1_b, ln2_g, ln2_b,
           ffn_w_gate, ffn_w_up, ffn_w_down, router_w, moe_w_gate, moe_w_up, moe_w_down):
    pb, seq, D = x_prompt.shape
    db, dseq, _ = x_sample.shape
    depth = w_in.shape[0]
    n_phys, page = cache_k.shape[1], cache_k.shape[2]
    n_pages = page_table.shape[1]
    W_b = H_B * D_HB
    C3 = 3 * H_A * DK_A
    alpha = (2.0 * depth) ** 0.25
    tp = pb * seq
    ts = db * SAMPLE_PAD

    xs_pad = jnp.pad(x_sample, ((0, 0), (0, SAMPLE_PAD - dseq), (0, 0)))
    x = jnp.concatenate([x_prompt.reshape(tp, D), xs_pad.reshape(ts, D)], axis=0)
    ck = jnp.transpose(cache_k, (0, 1, 3, 4, 2)).reshape(depth * n_phys, W_b, page)
    cv = jnp.transpose(cache_v, (0, 1, 3, 4, 2)).reshape(depth * n_phys, W_b, page)
    pt_flat = page_table.reshape(-1).astype(jnp.int32)
    buf0 = jnp.zeros((pb, CONV_W - 1, C3), F32)
    s00 = jnp.zeros((pb, H_A, DK_A, DV_A), F32)
    ffn_wg, ffn_wu, ffn_wd = _bf(ffn_w_gate), _bf(ffn_w_up), _bf(ffn_w_down)
    moe_wg, moe_wu, moe_wd = _bf(moe_w_gate), _bf(moe_w_up), _bf(moe_w_down)

    outs = {n: [] for n in ("kp", "vp", "sp", "cp", "ks", "vs", "ss", "cs")}
    for li in range(depth):
        w_main, wkv_t = pack_w_in(w_in[li])
        qkv, z, gla, glb, bd, qb = in_projection_main(x, w_main)
        kt32, vt32, kt16, vt16 = in_projection_kv(x, wkv_t, 0, tp, pb, seq, True)
        kb_s, vb_s = in_projection_kv(x, wkv_t, tp, ts, db, SAMPLE_PAD, False)
        head_par = jnp.zeros((2, LANE), F32)
        head_par = head_par.at[0, H_A:2 * H_A].set(-jnp.exp(a_log[li].astype(F32)))
        head_par = head_par.at[1, H_A:2 * H_A].set(dt_bias[li].astype(F32))
        norm_w = o_norm_w[li].reshape(1, DV_A).astype(F32)

        qp, kp_, vp_, cbuf_p = conv_prep(qkv, conv_w[li], buf0, 0, pb, seq, 512, seq)
        qs, ks_, vs_, cbuf_s = conv_prep(qkv, conv_w[li], state_conv[li], tp, db, SAMPLE_PAD,
                                         SAMPLE_PAD, dseq, G=16)
        oa_p, s_p = delta_rule_prompt(qp, kp_, vp_, bd, z, head_par, norm_w, s00, pb, seq,
                                      DELTA_CHUNK)
        oa_s, s_s = delta_rule_short(qs, ks_, vs_, bd, z, head_par, norm_w, state_delta[li],
                                     tp, db, SAMPLE_PAD, dseq)
        ob_p = sb_attention_prompt(qb, kt16, vt16, sb_bias[li].astype(F32), pb, seq)
        ob_s = sb_attention_sample(qb, kb_s, vb_s, sb_bias[li].astype(F32).reshape(H_B, 1), ck, cv,
                                   pt_flat, li * n_phys, tp, db, n_pages, dseq)
        oa = jnp.concatenate([oa_p, oa_s], axis=0)
        ob = jnp.concatenate([ob_p, _bf(ob_s)], axis=0)
        ln1 = jnp.stack([ln1_g[li], ln1_b[li]]).astype(F32)
        ln2 = jnp.stack([ln2_g[li], ln2_b[li]]).astype(F32)
        x1 = merge_out_ln(x, oa, ob, gla, glb, _bf(w_proj_a[li]), _bf(w_proj_b[li]), _bf(w_out[li]),
                          ln1, alpha)
        j = li // 2
        if li % 2 == 0:
            x = ffn_ln(x1, ffn_wg, ffn_wu, ffn_wd, j, ln2, alpha)
        else:
            gate = router_gate(x1, router_w[j].astype(F32))
            x = moe_ln(x1, gate, moe_wg, moe_wu, moe_wd, j, ln2, alpha)

        outs["kp"].append(jnp.transpose(kt32.reshape(pb, H_B, D_HB, seq), (0, 3, 1, 2)))
        outs["vp"].append(jnp.transpose(vt32.reshape(pb, H_B, D_HB, seq), (0, 3, 1, 2)))
        outs["sp"].append(s_p)
        outs["cp"].append(cbuf_p)
        outs["ks"].append(kb_s.reshape(db, SAMPLE_PAD, H_B, D_HB)[:, :dseq])
        outs["vs"].append(vb_s.reshape(db, SAMPLE_PAD, H_B, D_HB)[:, :dseq])
        outs["ss"].append(s_s)
        outs["cs"].append(cbuf_s)

    y_prompt = x[:tp].reshape(pb, seq, D)
    y_sample = x[tp:].reshape(db, SAMPLE_PAD, D)[:, :dseq]
    st = {n: jnp.stack(v) for n, v in outs.items()}
    return (y_prompt, y_sample, st["kp"], st["vp"], st["sp"], st["cp"],
            st["ks"], st["vs"], st["ss"], st["cs"])
```

```python
import functools
import math

import jax
import jax.numpy as jnp
from jax import lax
from jax.experimental import pallas as pl
from jax.experimental.pallas import tpu as pltpu

F32 = jnp.float32
BF16 = jnp.bfloat16

H_A = 8
DK_A = 64
DV_A = 64
H_B = 8
D_HB = 64
CONV_W = 4
DELTA_CHUNK = 64
LN_EPS = 1e-5
NORM_EPS = 1e-6
SAMPLE_PAD = 8
LANE = 128
VMEM_LIMIT = 56 * 1024 * 1024
LOG2E = 1.4426950408889634
SB_TQ = 512
SB_TK = 256


def _bf(x):
    return x.astype(BF16)


def _dot(a, b):
    return jnp.dot(a, b, preferred_element_type=F32)


def _dot_nt(a, b):
    return lax.dot_general(a, b, (((1,), (1,)), ((), ())), preferred_element_type=F32)


def _dot_tn(a, b):
    return lax.dot_general(a, b, (((0,), (0,)), ((), ())), preferred_element_type=F32)


def _split2(x):
    hi = _bf(x)
    lo = _bf(x - hi.astype(F32))
    return hi, lo


def _split3(x):
    hi = _bf(x)
    r = x - hi.astype(F32)
    mid = _bf(r)
    lo = _bf(r - mid.astype(F32))
    return hi, mid, lo


def _dot3s(a, b):
    return _dot(a[0], b[0]) + (_dot(a[0], b[1]) + _dot(a[1], b[0]))


def _dot3(a, b):
    return _dot3s(_split2(a), _split2(b))


def _silu(x):
    return x * (1.0 / (1.0 + jnp.exp(-x)))


def _sigmoid(x):
    return 1.0 / (1.0 + jnp.exp(-x))


def _softplus(x):
    return jnp.maximum(x, 0.0) + jnp.log1p(jnp.exp(-jnp.abs(x)))


def _layer_norm(x, g, b):
    mu = jnp.mean(x, axis=-1, keepdims=True)
    xc = x - mu
    var = jnp.mean(xc * xc, axis=-1, keepdims=True)
    return xc * lax.rsqrt(var + LN_EPS) * g + b


def _cparams(sem, vmem=VMEM_LIMIT):
    return pltpu.CompilerParams(dimension_semantics=sem, vmem_limit_bytes=vmem)


_PROJ_GROUPS = (("qkv", 1536, F32), ("z", 512, F32), ("gla", 1024, F32), ("glb", 1024, F32),
                ("bd", 128, F32), ("qb", 512, F32))
_PROJ_MAIN = sum(w for _, w, _ in _PROJ_GROUPS)
_W_KV = 2 * H_B * D_HB


def _proj_main_kernel(x_ref, w_ref, *o_refs):
    x = _bf(x_ref[...])
    off = 0
    for (_, width, dt), o_ref in zip(_PROJ_GROUPS, o_refs):
        o_ref[...] = _dot(x, w_ref[:, off:off + width]).astype(dt)
        off += width


def in_projection_main(x, w_main, tm=256):
    T, D = x.shape
    row = lambda i: (i, 0)
    return pl.pallas_call(
        _proj_main_kernel,
        grid=(T // tm,),
        in_specs=[pl.BlockSpec((tm, D), row), pl.BlockSpec((D, _PROJ_MAIN), lambda i: (0, 0))],
        out_specs=[pl.BlockSpec((tm, w), row) for _, w, _ in _PROJ_GROUPS],
        out_shape=[jax.ShapeDtypeStruct((T, w), dt) for _, w, dt in _PROJ_GROUPS],
        compiler_params=_cparams(("parallel",)),
        name="in_projection_main",
    )(x, w_main)


def _proj_kv_rows_kernel(x_ref, wkv_ref, k32, v32):
    x = _bf(x_ref[...])
    wb = H_B * D_HB
    k32[...] = _dot_nt(x, wkv_ref[:wb, :])
    v32[...] = _dot_nt(x, wkv_ref[wb:, :])


def in_projection_kv_rows(x, wkv_t, row0, nrows, tm=256):
    D = x.shape[1]
    W = H_B * D_HB
    rb0 = row0 // tm
    return pl.pallas_call(
        _proj_kv_rows_kernel,
        grid=(nrows // tm,),
        in_specs=[pl.BlockSpec((tm, D), lambda i: (rb0 + i, 0)),
                  pl.BlockSpec((_W_KV, D), lambda i: (0, 0))],
        out_specs=[pl.BlockSpec((tm, W), lambda i: (i, 0))] * 2,
        out_shape=[jax.ShapeDtypeStruct((nrows, W), F32)] * 2,
        compiler_params=_cparams(("parallel",)),
        name="in_projection_kv_rows",
    )(x, wkv_t)


def _proj_kv_t_kernel(x_ref, wkv_ref, *refs):
    k32, v32, kb16, vb16 = refs[-4:]
    x = _bf(x_ref[...])
    wb = H_B * D_HB
    kt = _dot_nt(wkv_ref[:wb, :], x)
    vt = _dot_nt(wkv_ref[wb:, :], x)
    k32[0, 0] = kt
    v32[0, 0] = vt
    kb16[0, 0] = _bf(kt)
    vb16[0, 0] = _bf(vt)


def in_projection_kv_t(x, wkv_t, nseq, seqlen, layer, depth, k_all=None, v_all=None, tm=SB_TK):
    D = x.shape[1]
    W = H_B * D_HB
    nt = seqlen // tm
    t32 = pl.BlockSpec((1, 1, W, tm), lambda i: (layer, i // nt, 0, i % nt))
    t16 = pl.BlockSpec((1, 1, W, tm), lambda i: (i // nt, i % nt, 0, 0))
    all_shape = jax.ShapeDtypeStruct((depth, nseq, W, seqlen), F32)
    in_specs = [pl.BlockSpec((tm, D), lambda i: (i, 0)),
                pl.BlockSpec((_W_KV, D), lambda i: (0, 0))]
    args = [x, wkv_t]
    aliases = {}
    if k_all is not None:
        in_specs += [pl.BlockSpec(memory_space=pl.ANY)] * 2
        args += [k_all, v_all]
        aliases = {2: 0, 3: 1}
    return pl.pallas_call(
        _proj_kv_t_kernel,
        grid=(nseq * nt,),
        in_specs=in_specs,
        out_specs=[t32, t32, t16, t16],
        out_shape=[all_shape, all_shape] + [jax.ShapeDtypeStruct((nseq, nt, W, tm), BF16)] * 2,
        input_output_aliases=aliases,
        compiler_params=_cparams(("parallel",)),
        name="in_projection_kv_t",
    )(*args)


def pack_w_in(w_in):
    c = 3 * H_A * DK_A
    wv = H_A * DV_A
    wb = H_B * D_HB
    D = w_in.shape[0]
    i = 0
    qkv = w_in[:, i:i + c]; i += c
    z = w_in[:, i:i + wv]; i += wv
    bd = w_in[:, i:i + 2 * H_A]; i += 2 * H_A
    qb = w_in[:, i:i + wb]; i += wb
    kv = w_in[:, i:i + 2 * wb]; i += 2 * wb
    gla = w_in[:, i:i + D]; i += D
    glb = w_in[:, i:i + D]; i += D
    bd = jnp.pad(bd, ((0, 0), (0, LANE - 2 * H_A)))
    main = _bf(jnp.concatenate([qkv, z, gla, glb, bd, qb], axis=1))
    return main, _bf(kv.T)


def _prep_kernel(x_ref, w_ref, buf_ref, hsum_ref, q_ref, k_ref, v_ref, nb_ref, xp_ref,
                 *, tt, valid_last, G):
    t = pl.program_id(1)
    nt = pl.num_programs(1)
    halo = CONV_W - 1
    span = tt + 8
    w = w_ref[...]
    ys = []
    for g in range(G):
        base = g * span + 8

        @pl.when(t == 0)
        def _():
            xp_ref[base - halo:base, :] = buf_ref[g]

        if G == 1:
            @pl.when(t > 0)
            def _():
                xp_ref[base - halo:base, :] = xp_ref[base + tt - halo:base + tt, :]

        xp_ref[base:base + tt, :] = x_ref[g * tt:(g + 1) * tt, :]
        y = xp_ref[base - halo:base - halo + tt, :] * w[0:1, :]
        for i in range(1, CONV_W):
            y = y + xp_ref[base - halo + i:base - halo + i + tt, :] * w[i:i + 1, :]
        ys.append(y)
    y = _silu(ys[0] if G == 1 else jnp.concatenate(ys, axis=0))
    wq = H_A * DK_A
    q = y[:, :wq]
    k = y[:, wq:2 * wq]
    v_ref[...] = y[:, 2 * wq:]
    hs = hsum_ref[...]

    def l2n(a):
        hi, lo = _split2(a * a)
        ss = _dot(hi, hs) + _dot(lo, hs)
        return a * lax.rsqrt(ss + NORM_EPS)

    q_ref[...] = l2n(q) * (DK_A ** -0.5)
    k_ref[...] = l2n(k)

    @pl.when(t == nt - 1)
    def _():
        for g in range(G):
            base = g * span + 8
            nb_ref[g] = xp_ref[base + valid_last - halo:base + valid_last, :]


def conv_prep(qkv, conv_w, buf, row0, nseq, seqlen, tt, valid_len, G=1):
    C = qkv.shape[1]
    T = nseq * seqlen
    W = C // 3
    nt = seqlen // tt
    assert G == 1 or nt == 1
    R = G * tt
    rb0 = row0 // R
    valid_last = valid_len - (nt - 1) * tt
    head = jnp.arange(W, dtype=jnp.int32) // DK_A
    hsum = (head[:, None] == head[None, :]).astype(BF16)
    row_map = lambda b, t: (b * nt + t, 0)
    kern = functools.partial(_prep_kernel, tt=tt, valid_last=valid_last, G=G)
    return pl.pallas_call(
        kern,
        grid=(nseq // G, nt),
        in_specs=[pl.BlockSpec((R, C), lambda b, t: (rb0 + b * nt + t, 0)),
                  pl.BlockSpec((CONV_W, C), lambda b, t: (0, 0)),
                  pl.BlockSpec((G, CONV_W - 1, C), lambda b, t: (b, 0, 0)),
                  pl.BlockSpec((W, W), lambda b, t: (0, 0))],
        out_specs=[pl.BlockSpec((R, W), row_map),
                   pl.BlockSpec((R, W), row_map),
                   pl.BlockSpec((R, W), row_map),
                   pl.BlockSpec((G, CONV_W - 1, C), lambda b, t: (b, 0, 0))],
        out_shape=[jax.ShapeDtypeStruct((T, W), F32)] * 3
                  + [jax.ShapeDtypeStruct((nseq, CONV_W - 1, C), F32)],
        scratch_shapes=[pltpu.VMEM((G * (tt + 8), C), F32)],
        compiler_params=_cparams(("parallel", "arbitrary")),
        name="conv_prep",
    )(qkv, conv_w, buf, hsum)


def _delta_gates(bd, hp, C, valid_len, tri_incl):
    beta_all = _sigmoid(bd)
    g_all = hp[0:1, :] * _softplus(bd + hp[1:2, :])
    if valid_len < C:
        pos = lax.broadcasted_iota(jnp.int32, bd.shape, 0) % C
        live = pos < valid_len
        beta_all = jnp.where(live, beta_all, 0.0)
        g_all = jnp.where(live, g_all, 0.0)
    gcs = []
    for g in range(bd.shape[0] // C):
        gh, gm, gl = _split3(g_all[g * C:(g + 1) * C, :])
        gcs.append(_dot(tri_incl, gh) + (_dot(tri_incl, gm) + _dot(tri_incl, gl)))
    return beta_all, gcs


def _delta_intra(qs, ks, vs, betas, gcs, C, nstage):
    n = len(qs)
    row = lax.broadcasted_iota(jnp.int32, (C, C), 0)
    col = lax.broadcasted_iota(jnp.int32, (C, C), 1)
    incl = row >= col
    strict = row > col
    eye = row == col
    ones_cc = jnp.ones((C, C), BF16)
    rng = range(n)
    a_mats = [jnp.broadcast_to(gcs[i], (C, C)) for i in rng]
    dsp = [_split3(jnp.where(eye, a_mats[i], 0.0)) for i in rng]
    b_mats = [_dot_nt(ones_cc, dsp[i][0]) + (_dot_nt(ones_cc, dsp[i][1]) + _dot_nt(ones_cc, dsp[i][2]))
              for i in rng]
    decays = [jnp.where(incl, jnp.exp(jnp.minimum(a_mats[i] - b_mats[i], 0.0)), 0.0) for i in rng]
    kbs = [ks[i] * betas[i] for i in rng]
    kks = [_dot_nt(_bf(kbs[i]), _bf(ks[i])) for i in rng]
    qks = [_dot_nt(_bf(qs[i]), _bf(ks[i])) for i in rng]
    egs = [jnp.exp(gcs[i]) for i in rng]
    ps = [_split2(-jnp.where(strict, kks[i] * decays[i], 0.0)) for i in rng]
    rs = [jnp.concatenate([vs[i] * betas[i], kbs[i] * egs[i]], axis=-1) for i in rng]
    for s in range(min(2, nstage)):
        if s > 0:
            pn = [_dot3s(ps[i], ps[i]) for i in rng]
            ps = [_split2(pn[i]) for i in rng]
        rsp = [_split2(rs[i]) for i in rng]
        dr = [_dot3s(ps[i], rsp[i]) for i in rng]
        rs = [rs[i] + dr[i] for i in rng]
    pb = [ps[i][0] for i in rng]
    for _ in range(nstage - 2):
        pb = [_bf(_dot(pb[i], pb[i])) for i in rng]
        dr = [_dot(pb[i], _bf(rs[i])) for i in rng]
        rs = [rs[i] + dr[i] for i in rng]
    us = [rs[i][:, :DV_A] for i in rng]
    ws = [rs[i][:, DV_A:] for i in rng]
    qkm = [jnp.where(incl, qks[i] * decays[i], 0.0) for i in rng]
    kdecs = [ks[i] * jnp.exp(gcs[i][C - 1:C, :] - gcs[i]) for i in rng]
    qgs = [qs[i] * egs[i] for i in rng]
    return us, ws, qkm, kdecs, qgs


def _delta_state(us, ws, qkm, kdecs, qgs, egl, states):
    rng = range(len(us))
    sb = [_bf(states[i]) for i in rng]
    ws_s = [_dot(ws[i], sb[i]) for i in rng]
    qs_s = [_dot(qgs[i], sb[i]) for i in rng]
    vn = [_bf(us[i] - ws_s[i]) for i in rng]
    os_ = [qs_s[i] + _dot(qkm[i], vn[i]) for i in rng]
    new = [states[i] * egl[i] + _dot_tn(kdecs[i], vn[i]) for i in rng]
    return os_, new


def _gated_rms(o, nw, z):
    rn = o * lax.rsqrt(jnp.mean(o * o, axis=-1, keepdims=True) + NORM_EPS)
    return rn * nw * _silu(z)


def _delta_intra_kernel(q_ref, k_ref, v_ref, bd_ref, hp_ref,
                        u_ref, w_ref, qk_ref, kd_ref, qg_ref, egl_ref, *, C, G, nstage):
    row = lax.broadcasted_iota(jnp.int32, (C, C), 0)
    col = lax.broadcasted_iota(jnp.int32, (C, C), 1)
    tri_incl = jnp.where(row >= col, 1.0, 0.0).astype(BF16)
    beta_all, gcs = _delta_gates(bd_ref[...], hp_ref[...], C, C, tri_incl)
    qs, ks, vs, betas, gch = [], [], [], [], []
    for g in range(G):
        rows = slice(g * C, (g + 1) * C)
        for h in range(H_A):
            sl = slice(h * DK_A, (h + 1) * DK_A)
            qs.append(q_ref[rows, sl])
            ks.append(k_ref[rows, sl])
            vs.append(v_ref[rows, sl])
            betas.append(beta_all[rows, h:h + 1])
            gch.append(gcs[g][:, H_A + h:H_A + h + 1])
    us, ws, qkm, kdecs, qgs = _delta_intra(qs, ks, vs, betas, gch, C, nstage)
    for g in range(G):
        rows = slice(g * C, (g + 1) * C)
        pr = slice(g * H_A, (g + 1) * H_A)
        u_ref[rows, :] = jnp.concatenate(us[pr], axis=-1)
        w_ref[rows, :] = _bf(jnp.concatenate(ws[pr], axis=-1))
        qk_ref[rows, :] = _bf(jnp.concatenate(qkm[pr], axis=-1))
        kd_ref[rows, :] = _bf(jnp.concatenate(kdecs[pr], axis=-1))
        qg_ref[rows, :] = _bf(jnp.concatenate(qgs[pr], axis=-1))
        egl_ref[0, g] = jnp.exp(gcs[g][C - 1:C, :])


def _delta_seq_kernel(u_ref, w_ref, qk_ref, kd_ref, qg_ref, egl_ref, nw_ref, s0_ref, *rest, nb):
    z_refs = rest[:nb]
    o_ref, sf_ref, s_scr = rest[nb:]
    c = pl.program_id(0)
    nc = pl.num_programs(0)

    @pl.when(c == 0)
    def _():
        s_scr[...] = s0_ref[...]

    nw = nw_ref[...]
    us, ws, qkm, kdecs, qgs, egl, states = [], [], [], [], [], [], []
    for b in range(nb):
        er = egl_ref[b, 0]
        for h in range(H_A):
            sl = slice(h * DK_A, (h + 1) * DK_A)
            us.append(u_ref[b, :, sl])
            ws.append(w_ref[b, :, sl])
            qkm.append(qk_ref[b, :, sl])
            kdecs.append(kd_ref[b, :, sl])
            qgs.append(qg_ref[b, :, sl])
            egl.append(er[:, H_A + h:H_A + h + 1])
            states.append(s_scr[b, h])
    os_, new = _delta_state(us, ws, qkm, kdecs, qgs, egl, states)
    for b in range(nb):
        outs = []
        for h in range(H_A):
            i = b * H_A + h
            s_scr[b, h] = new[i]
            outs.append(_gated_rms(os_[i], nw, z_refs[b][:, h * DV_A:(h + 1) * DV_A]))
        o_ref[b] = _bf(jnp.concatenate(outs, axis=-1))

    @pl.when(c == nc - 1)
    def _():
        sf_ref[...] = s_scr[...]


def delta_rule_prompt(q, k, v, bd, z, head_par, norm_w, s0, nseq, seqlen, C, G=4):
    T, W = q.shape
    nc = seqlen // C
    nstage = max(1, int(math.log2(C)))
    R = G * C
    ncb = nc // G
    row = lambda i: (i, 0)
    intra = pl.pallas_call(
        functools.partial(_delta_intra_kernel, C=C, G=G, nstage=nstage),
        grid=(T // R,),
        in_specs=[pl.BlockSpec((R, W), row)] * 3
                 + [pl.BlockSpec((R, LANE), row), pl.BlockSpec((2, LANE), lambda i: (0, 0))],
        out_specs=[pl.BlockSpec((R, W), row)] * 5
                  + [pl.BlockSpec((1, G, 1, LANE), lambda i: (i // ncb, i % ncb, 0, 0))],
        out_shape=[jax.ShapeDtypeStruct((T, W), F32)] + [jax.ShapeDtypeStruct((T, W), BF16)] * 4
                  + [jax.ShapeDtypeStruct((nseq, nc, 1, LANE), F32)],
        compiler_params=_cparams(("parallel",)),
        name="delta_intra",
    )
    u, w, qk, kd, qg, egl = intra(q, k, v, bd, head_par)
    r3 = lambda a: a.reshape(nseq, seqlen, W)
    blk = pl.BlockSpec((nseq, C, W), lambda c: (0, c, 0))
    st = pl.BlockSpec((nseq, H_A, DK_A, DV_A), lambda c: (0, 0, 0, 0))
    o, sf = pl.pallas_call(
        functools.partial(_delta_seq_kernel, nb=nseq),
        grid=(nc,),
        in_specs=[blk] * 5 + [pl.BlockSpec((nseq, 1, 1, LANE), lambda c: (0, c, 0, 0)),
                              pl.BlockSpec((1, DV_A), lambda c: (0, 0)), st]
                 + [pl.BlockSpec((C, W), functools.partial(lambda c, b: (b * nc + c, 0), b=b))
                    for b in range(nseq)],
        out_specs=[blk, st],
        out_shape=[jax.ShapeDtypeStruct((nseq, seqlen, W), BF16),
                   jax.ShapeDtypeStruct((nseq, H_A, DK_A, DV_A), F32)],
        scratch_shapes=[pltpu.VMEM((nseq, H_A, DK_A, DV_A), F32)],
        compiler_params=_cparams(("arbitrary",)),
        name="delta_seq",
    )(r3(u), r3(w), r3(qk), r3(kd), r3(qg), egl, norm_w, s0, *([z] * nseq))
    return o.reshape(T, W), sf


def _delta_short_kernel(q_ref, k_ref, v_ref, bd_ref, z_ref, hp_ref, nw_ref, s0_ref,
                        o_ref, sf_ref, *, C, G, valid_len, nstage):
    row = lax.broadcasted_iota(jnp.int32, (C, C), 0)
    col = lax.broadcasted_iota(jnp.int32, (C, C), 1)
    tri_incl = jnp.where(row >= col, 1.0, 0.0).astype(BF16)
    beta_all, gcs = _delta_gates(bd_ref[...], hp_ref[...], C, valid_len, tri_incl)
    qs, ks, vs, betas, gch, egl, states = [], [], [], [], [], [], []
    for g in range(G):
        rows = slice(g * C, (g + 1) * C)
        for h in range(H_A):
            sl = slice(h * DK_A, (h + 1) * DK_A)
            qs.append(q_ref[rows, sl])
            ks.append(k_ref[rows, sl])
            vs.append(v_ref[rows, sl])
            betas.append(beta_all[rows, h:h + 1])
            gc = gcs[g][:, H_A + h:H_A + h + 1]
            gch.append(gc)
            egl.append(jnp.exp(gc[C - 1:C, :]))
            states.append(s0_ref[g, h])
    us, ws, qkm, kdecs, qgs = _delta_intra(qs, ks, vs, betas, gch, C, nstage)
    os_, new = _delta_state(us, [_bf(a) for a in ws], [_bf(a) for a in qkm],
                            [_bf(a) for a in kdecs], [_bf(a) for a in qgs], egl, states)
    nw = nw_ref[...]
    for g in range(G):
        rows = slice(g * C, (g + 1) * C)
        outs = []
        for h in range(H_A):
            i = g * H_A + h
            sf_ref[g, h] = new[i]
            outs.append(_gated_rms(os_[i], nw, z_ref[rows, h * DV_A:(h + 1) * DV_A]))
        o_ref[rows, :] = _bf(jnp.concatenate(outs, axis=-1))


def delta_rule_short(q, k, v, bd, z, head_par, norm_w, s0, row0, nseq, C, valid_len, G=8):
    T, W = q.shape
    nstage = max(1, int(math.log2(C)))
    R = G * C
    rb0 = row0 // R
    row = lambda i: (i, 0)
    off = lambda i: (rb0 + i, 0)
    st = pl.BlockSpec((G, H_A, DK_A, DV_A), lambda i: (i, 0, 0, 0))
    return pl.pallas_call(
        functools.partial(_delta_short_kernel, C=C, G=G, valid_len=valid_len, nstage=nstage),
        grid=(nseq // G,),
        in_specs=[pl.BlockSpec((R, W), row)] * 3
                 + [pl.BlockSpec((R, LANE), off), pl.BlockSpec((R, W), off),
                    pl.BlockSpec((2, LANE), lambda i: (0, 0)),
                    pl.BlockSpec((1, DV_A), lambda i: (0, 0)), st],
        out_specs=[pl.BlockSpec((R, W), row), st],
        out_shape=[jax.ShapeDtypeStruct((T, W), BF16),
                   jax.ShapeDtypeStruct((nseq, H_A, DK_A, DV_A), F32)],
        compiler_params=_cparams(("parallel",)),
        name="delta_short",
    )(q, k, v, bd, z, head_par, norm_w, s0)


def _softplus2(z2):
    return jnp.maximum(jnp.log(1.0 + jnp.exp2(jnp.minimum(z2, 64.0))) * LOG2E, z2)


def _sb_prompt_kernel(bias_ref, q_ref, kt_ref, vt_ref, o_ref, *, tq, tk):
    hp = pl.program_id(1)
    i = pl.program_id(2)
    lane = lax.broadcasted_iota(jnp.int32, (tq, LANE), 1)
    first = lane < D_HB
    q = _bf(q_ref[...])
    zero = jnp.zeros_like(q)
    qm = (jnp.where(first, q, zero), jnp.where(first, zero, q))
    scale2 = (D_HB ** -0.5) * LOG2E
    bias2 = (bias_ref[2 * hp] * LOG2E, bias_ref[2 * hp + 1] * LOG2E)
    r = lax.broadcasted_iota(jnp.int32, (tk, tk + LANE), 0)
    c = lax.broadcasted_iota(jnp.int32, (tk, tk + LANE), 1)
    suffix = jnp.where(jnp.logical_or(c >= tk, r > c), -1.0, 0.0).astype(BF16)
    nrep = tk // LANE
    heads = (0, 1)

    def tile(j, carry, acc, masked, row0=0):
        kt = kt_ref[0, j]
        vt = vt_ref[0, j]
        live = slice(row0, tq)
        rows = tq - row0
        zs = [_dot(qm[h][live], kt) * scale2 + bias2[h] for h in heads]
        sps = [_softplus2(zs[h]) for h in heads]
        if masked:
            qpos = i * tq + row0 + lax.broadcasted_iota(jnp.int32, (rows, tk), 0)
            kpos = j * tk + lax.broadcasted_iota(jnp.int32, (rows, tk), 1)
            mask = kpos < qpos
            spm = [jnp.where(mask, sps[h], 0.0) for h in heads]
        else:
            spm = sps
        cm = [_dot(_bf(spm[h]), suffix) for h in heads]
        ex = [zs[h] - sps[h] + (cm[h][:, :tk] + jnp.concatenate([carry[h][live]] * nrep, axis=1))
              for h in heads]
        av = [jnp.exp2(ex[h]) for h in heads]
        if masked:
            av = [jnp.where(mask, av[h], 0.0) for h in heads]
        pv = [_dot_nt(_bf(av[h]), vt) for h in heads]
        new_c = [carry[h][live] + cm[h][:, tk:] for h in heads]
        head0 = lax.broadcasted_iota(jnp.int32, (rows, LANE), 1) < D_HB
        new_a = acc[live] + jnp.where(head0, pv[0], pv[1])
        if row0:
            new_c = [jnp.concatenate([carry[h][:row0], new_c[h]], axis=0) for h in heads]
            new_a = jnp.concatenate([acc[:row0], new_a], axis=0)
        return tuple(new_c), new_a

    zc = jnp.zeros((tq, LANE), F32)
    carry, acc = (zc, zc), zc
    nd = tq // tk
    j0 = i * nd
    for dj in reversed(range(nd)):
        carry, acc = tile(j0 + dj, carry, acc, True, row0=dj * tk)

    def body(jj, st):
        carry, acc = (st[0], st[1]), st[2]
        for d in range(nd):
            carry, acc = tile(j0 - 1 - jj * nd - d, carry, acc, False)
        return carry[0], carry[1], acc

    _, _, acc = lax.fori_loop(0, i, body, (carry[0], carry[1], acc))
    o_ref[...] = _bf(acc)


def sb_attention_prompt(qb, kt16, vt16, sb_bias, nseq, seqlen, tq=SB_TQ):
    W = qb.shape[1]
    nt, tk = kt16.shape[1], kt16.shape[3]
    nq = seqlen // tq
    nhp = W // LANE
    kern = functools.partial(_sb_prompt_kernel, tq=tq, tk=tk)
    kv_spec = pl.BlockSpec((1, nt, LANE, tk), lambda b, hp, i: (b, 0, hp, 0))
    return pl.pallas_call(
        kern,
        grid=(nseq, nhp, nq),
        in_specs=[pl.BlockSpec(memory_space=pltpu.SMEM),
                  pl.BlockSpec((tq, LANE), lambda b, hp, i: (b * nq + i, hp)),
                  kv_spec, kv_spec],
        out_specs=pl.BlockSpec((tq, LANE), lambda b, hp, i: (b * nq + i, hp)),
        out_shape=jax.ShapeDtypeStruct((nseq * seqlen, W), BF16),
        compiler_params=_cparams(("parallel", "parallel", "arbitrary")),
        name="sb_attention_prompt",
    )(sb_bias, qb, kt16, vt16)


def _strict_lower_ones(n):
    r = lax.broadcasted_iota(jnp.int32, (n, n), 0)
    c = lax.broadcasted_iota(jnp.int32, (n, n), 1)
    return jnp.where(r > c, 1.0, 0.0).astype(BF16)


def _sb_sample_kernel(pt_ref, q_ref, ko_ref, vo_ref, bias_ref, *rest, n_pages, n_tok, page):
    k_refs = rest[:n_pages]
    v_refs = rest[n_pages:2 * n_pages]
    o_ref = rest[2 * n_pages]
    W = H_B * D_HB
    R = n_tok * H_B
    hrow = lax.broadcasted_iota(jnp.int32, (H_B, W), 0)
    hlane = lax.broadcasted_iota(jnp.int32, (H_B, W), 1) // D_HB
    hmask = hrow == hlane
    q = q_ref[...]
    qm = _bf(jnp.concatenate(
        [jnp.where(hmask, jnp.broadcast_to(q[t:t + 1, :], (H_B, W)), 0.0) for t in range(n_tok)],
        axis=0))
    bias2 = jnp.concatenate([bias_ref[...]] * n_tok, axis=0) * LOG2E
    scale2 = (D_HB ** -0.5) * LOG2E
    no = ko_ref.shape[0]
    trow = lax.broadcasted_iota(jnp.int32, (R, no), 0) // H_B
    scol = lax.broadcasted_iota(jnp.int32, (R, no), 1)
    own_mask = scol < trow
    order = list(reversed(range(n_pages)))
    zs = [_dot_nt(qm, _bf(ko_ref[...]))] + [_dot(qm, _bf(k_refs[p][0])) for p in order]
    zs = [z * scale2 + bias2 for z in zs]
    sps = [_softplus2(z) for z in zs]
    spb = [_bf(jnp.where(own_mask, sps[0], 0.0))] + [_bf(sp) for sp in sps[1:]]
    r = lax.broadcasted_iota(jnp.int32, (page, page + LANE), 0)
    c = lax.broadcasted_iota(jnp.int32, (page, page + LANE), 1)
    suffix = jnp.where(jnp.logical_or(c >= page, r > c), -1.0, 0.0).astype(BF16)
    own_local = _dot(spb[0], -_strict_lower_ones(no))
    own_total = _dot(spb[0], jnp.full((no, LANE), -1.0, BF16))
    cms = [_dot(sp, suffix) for sp in spb[1:]]
    carry = own_total
    exs = [zs[0] - sps[0] + own_local]
    for t, cm in enumerate(cms):
        exs.append(zs[t + 1] - sps[t + 1] + (cm[:, :page] + carry))
        carry = carry + cm[:, page:]
    avs = [jnp.exp2(ex) for ex in exs]
    acc = _dot(_bf(jnp.where(own_mask, avs[0], 0.0)), _bf(vo_ref[...]))
    for t, p in enumerate(order):
        acc = acc + _dot_nt(_bf(avs[t + 1]), _bf(v_refs[p][0]))
    rows = [jnp.sum(jnp.where(hmask, acc[t * H_B:(t + 1) * H_B, :], 0.0), axis=0, keepdims=True)
            for t in range(n_tok)]
    rows.append(jnp.zeros((q.shape[0] - n_tok, W), F32))
    o_ref[...] = jnp.concatenate(rows, axis=0)


def sb_attention_sample(qb, kb, vb, sb_bias_col, cache_k, cache_v, page_table_flat, layer,
                        row0, nseq, n_pages, n_tok):
    W = qb.shape[1]
    page = cache_k.shape[2]
    rb0 = row0 // SAMPLE_PAD
    own_q = pl.BlockSpec((SAMPLE_PAD, W), lambda b, pt: (rb0 + b, 0))
    own = pl.BlockSpec((SAMPLE_PAD, W), lambda b, pt: (b, 0))

    def page_spec(p):
        return pl.BlockSpec((1, W, page), lambda b, pt: (layer + pt[b * n_pages + p], 0, 0))

    kern = functools.partial(_sb_sample_kernel, n_pages=n_pages, n_tok=n_tok, page=page)
    grid_spec = pltpu.PrefetchScalarGridSpec(
        num_scalar_prefetch=1,
        grid=(nseq,),
        in_specs=[own_q, own, own, pl.BlockSpec((H_B, 1), lambda b, pt: (0, 0))]
                 + [page_spec(p) for p in range(n_pages)] * 2,
        out_specs=pl.BlockSpec((SAMPLE_PAD, W), lambda b, pt: (b, 0)),
    )
    return pl.pallas_call(
        kern,
        grid_spec=grid_spec,
        out_shape=jax.ShapeDtypeStruct((nseq * SAMPLE_PAD, W), F32),
        compiler_params=_cparams(("arbitrary",)),
        name="sb_attention_sample",
    )(page_table_flat, qb, kb, vb, sb_bias_col, *([cache_k] * n_pages), *([cache_v] * n_pages))


def _merge_kernel(x_ref, oa_ref, ob_ref, gla_ref, glb_ref, wa_ref, wb_ref, wo_ref, ln_ref,
                  o_ref, *, alpha):
    pa = _dot(oa_ref[...], wa_ref[...])
    pb = _dot(ob_ref[...], wb_ref[...])
    m = _sigmoid(gla_ref[...]) * pa + _sigmoid(glb_ref[...]) * pb
    y = _dot(_bf(m), wo_ref[...])
    ln = ln_ref[...]
    o_ref[...] = _layer_norm(alpha * x_ref[...] + y, ln[0:1, :], ln[1:2, :])


def merge_out_ln(x, oa, ob, gla, glb, wa, wb, wo, ln, alpha, tm=256):
    T, D = x.shape
    W = oa.shape[1]
    row = lambda i: (i, 0)
    full = lambda i: (0, 0)
    return pl.pallas_call(
        functools.partial(_merge_kernel, alpha=alpha),
        grid=(T // tm,),
        in_specs=[pl.BlockSpec((tm, D), row), pl.BlockSpec((tm, W), row), pl.BlockSpec((tm, W), row),
                  pl.BlockSpec((tm, D), row), pl.BlockSpec((tm, D), row),
                  pl.BlockSpec((W, D), full), pl.BlockSpec((W, D), full), pl.BlockSpec((D, D), full),
                  pl.BlockSpec((2, D), full)],
        out_specs=pl.BlockSpec((tm, D), row),
        out_shape=jax.ShapeDtypeStruct((T, D), F32),
        compiler_params=_cparams(("parallel",)),
        name="merge_out_ln",
    )(x, oa, ob, gla, glb, wa, wb, wo, ln)


def _router_kernel(x_ref, w_ref, g_ref, *, n_exp):
    logits = _dot3(x_ref[...], w_ref[...])
    lane = lax.broadcasted_iota(jnp.int32, logits.shape, 1)
    neg = jnp.float32(-jnp.inf)
    l1 = jnp.where(lane < n_exp, logits, neg)
    m1 = jnp.max(l1, axis=-1, keepdims=True)
    i1 = jnp.min(jnp.where(l1 == m1, lane, LANE), axis=-1, keepdims=True)
    l2 = jnp.where(lane == i1, neg, l1)
    m2 = jnp.max(l2, axis=-1, keepdims=True)
    i2 = jnp.min(jnp.where(l2 == m2, lane, LANE), axis=-1, keepdims=True)
    e = jnp.exp(m2 - m1)
    w1 = 1.0 / (1.0 + e)
    g_ref[...] = jnp.where(lane == i1, w1, 0.0) + jnp.where(lane == i2, e * w1, 0.0)


def router_gate(x, router_w, tm=512):
    T, D = x.shape
    n_exp = router_w.shape[1]
    wpad = jnp.pad(router_w, ((0, 0), (0, LANE - n_exp)))
    return pl.pallas_call(
        functools.partial(_router_kernel, n_exp=n_exp),
        grid=(T // tm,),
        in_specs=[pl.BlockSpec((tm, D), lambda i: (i, 0)), pl.BlockSpec((D, LANE), lambda i: (0, 0))],
        out_specs=pl.BlockSpec((tm, LANE), lambda i: (i, 0)),
        out_shape=jax.ShapeDtypeStruct((T, LANE), F32),
        compiler_params=_cparams(("parallel",)),
        name="router_gate",
    )(x, wpad)


def _ffn_kernel(x_ref, wg_ref, wu_ref, wd_ref, ln_ref, o_ref, acc_ref, *, alpha):
    f = pl.program_id(1)

    @pl.when(f == 0)
    def _():
        acc_ref[...] = jnp.zeros_like(acc_ref)

    x = x_ref[...]
    xb = _bf(x)
    h = _silu(_dot(xb, wg_ref[0])) * _dot(xb, wu_ref[0])
    acc_ref[...] += _dot(_bf(h), wd_ref[0])

    @pl.when(f == pl.num_programs(1) - 1)
    def _():
        ln = ln_ref[...]
        o_ref[...] = _layer_norm(alpha * x + acc_ref[...], ln[0:1, :], ln[1:2, :])


def ffn_ln(x, wg, wu, wd, j, ln, alpha, tm=512):
    T, D = x.shape
    F = wg.shape[2]
    tf = next(c for c in (1408, 1024, 512, 256, 128) if F % c == 0)
    row = lambda i, f: (i, 0)
    return pl.pallas_call(
        functools.partial(_ffn_kernel, alpha=alpha),
        grid=(T // tm, F // tf),
        in_specs=[pl.BlockSpec((tm, D), row),
                  pl.BlockSpec((1, D, tf), lambda i, f: (j, 0, f)),
                  pl.BlockSpec((1, D, tf), lambda i, f: (j, 0, f)),
                  pl.BlockSpec((1, tf, D), lambda i, f: (j, f, 0)),
                  pl.BlockSpec((2, D), lambda i, f: (0, 0))],
        out_specs=pl.BlockSpec((tm, D), row),
        out_shape=jax.ShapeDtypeStruct((T, D), F32),
        scratch_shapes=[pltpu.VMEM((tm, D), F32)],
        compiler_params=_cparams(("parallel", "arbitrary")),
        name="ffn_ln",
    )(x, wg, wu, wd, ln)


MOE_TOKENS = 1024
MOE_ROWS = 256
MOE_TAIL_ROWS = 128


def _moe_kernel(cnt_ref, x_ref, rkt_ref, rk_ref, gate_ref, wg_ref, wu_ref, wd_ref, ln_ref, o_ref,
                xb_ref, xs_ref, ya_ref, acc_ref, *, alpha, n_exp):
    i = pl.program_id(0)
    e = pl.program_id(1)
    f = pl.program_id(2)
    nf = pl.num_programs(2)
    TB, D = x_ref.shape
    cnt = cnt_ref[i * n_exp + e]
    n_full = cnt // MOE_ROWS
    rem = cnt - n_full * MOE_ROWS

    def for_tiles(fn):
        def body(s, carry):
            fn(pl.multiple_of(s * MOE_ROWS, MOE_ROWS), MOE_ROWS)
            return carry

        lax.fori_loop(0, n_full, body, 0)
        r_tail = pl.multiple_of(n_full * MOE_ROWS, MOE_ROWS)

        @pl.when(rem > MOE_TAIL_ROWS)
        def _():
            fn(r_tail, MOE_ROWS)

        @pl.when(jnp.logical_and(rem > 0, rem <= MOE_TAIL_ROWS))
        def _():
            fn(r_tail, MOE_TAIL_ROWS)

    @pl.when(jnp.logical_and(e == 0, f == 0))
    def _():
        acc_ref[...] = jnp.zeros_like(acc_ref)
        xb_ref[...] = _bf(x_ref[...])

    @pl.when(f == 0)
    def _():
        rke = rkt_ref[0, pl.ds(e, 1), :]

        def gather(r0, rows):
            ridx = lax.broadcasted_iota(jnp.int32, (rows, TB), 0).astype(F32)
            onehot = jnp.where(rke - r0.astype(F32) == ridx, 1.0, 0.0).astype(BF16)
            xs_ref[pl.ds(r0, rows), :] = _bf(_dot(onehot, xb_ref[...]))
            ya_ref[pl.ds(r0, rows), :] = jnp.zeros((rows, D), F32)

        for_tiles(gather)

    def expert(r0, rows):
        xsub = xs_ref[pl.ds(r0, rows), :]
        h = _silu(_dot(xsub, wg_ref[0, 0])) * _dot(xsub, wu_ref[0, 0])
        ya_ref[pl.ds(r0, rows), :] += _dot(_bf(h), wd_ref[0, 0])

    for_tiles(expert)

    @pl.when(f == nf - 1)
    def _():
        lane = lax.broadcasted_iota(jnp.int32, (TB, LANE), 1)
        ge = jnp.sum(jnp.where(lane == e, gate_ref[...], 0.0), axis=-1, keepdims=True)
        rkc = jnp.sum(jnp.where(lane == e, rk_ref[...], 0.0), axis=-1, keepdims=True)

        def combine(r0, rows):
            cidx = lax.broadcasted_iota(jnp.int32, (TB, rows), 1).astype(F32)
            onehot_t = jnp.where(rkc - r0.astype(F32) == cidx, 1.0, 0.0).astype(BF16)
            acc_ref[...] += ge * _dot(onehot_t, _bf(ya_ref[pl.ds(r0, rows), :]))

        for_tiles(combine)

    @pl.when(jnp.logical_and(e == n_exp - 1, f == nf - 1))
    def _():
        ln = ln_ref[...]
        o_ref[...] = _layer_norm(alpha * x_ref[...] + acc_ref[...], ln[0:1, :], ln[1:2, :])


def moe_ln(x, gate, wg, wu, wd, j, ln, alpha, tf=512):
    T, D = x.shape
    _, E, _, F = wg.shape
    TB = MOE_TOKENS
    nb = T // TB
    selb = (gate[:, :E] > 0.0).reshape(nb, TB, E)
    rank = jnp.cumsum(selb.astype(jnp.int32), axis=1) - 1
    rank = jnp.where(selb, rank, -1).astype(F32)
    cnt = jnp.sum(selb.astype(jnp.int32), axis=1).reshape(-1)
    rk = jnp.pad(rank.reshape(T, E), ((0, 0), (0, LANE - E)), constant_values=-1.0)
    rkt = jnp.transpose(rank, (0, 2, 1))
    blk = lambda i, e, f, cnt: (i, 0)
    grid_spec = pltpu.PrefetchScalarGridSpec(
        num_scalar_prefetch=1,
        grid=(nb, E, F // tf),
        in_specs=[pl.BlockSpec((TB, D), blk),
                  pl.BlockSpec((1, E, TB), lambda i, e, f, cnt: (i, 0, 0)),
                  pl.BlockSpec((TB, LANE), blk), pl.BlockSpec((TB, LANE), blk),
                  pl.BlockSpec((1, 1, D, tf), lambda i, e, f, cnt: (j, e, 0, f)),
                  pl.BlockSpec((1, 1, D, tf), lambda i, e, f, cnt: (j, e, 0, f)),
                  pl.BlockSpec((1, 1, tf, D), lambda i, e, f, cnt: (j, e, f, 0)),
                  pl.BlockSpec((2, D), lambda i, e, f, cnt: (0, 0))],
        out_specs=pl.BlockSpec((TB, D), blk),
        scratch_shapes=[pltpu.VMEM((TB, D), BF16), pltpu.VMEM((TB, D), BF16),
                        pltpu.VMEM((TB, D), F32), pltpu.VMEM((TB, D), F32)],
    )
    return pl.pallas_call(
        functools.partial(_moe_kernel, alpha=alpha, n_exp=E),
        grid_spec=grid_spec,
        out_shape=jax.ShapeDtypeStruct((T, D), F32),
        compiler_params=_cparams(("parallel", "arbitrary", "arbitrary")),
        name="moe_ln",
    )(cnt, x, rkt, rk, gate, wg, wu, wd, ln)


def kernel(x_prompt, x_sample, cache_k, cache_v, state_delta, state_conv, page_table, w_in, conv_w,
           a_log, dt_bias, o_norm_w, sb_bias, w_proj_a, w_proj_b, w_out, ln1_g, ln1_b, ln2_g, ln2_b,
           ffn_w_gate, ffn_w_up, ffn_w_down, router_w, moe_w_gate, moe_w_up, moe_w_down):
    pb, seq, D = x_prompt.shape
    db, dseq, _ = x_sample.shape
    depth = w_in.shape[0]
    n_phys, page = cache_k.shape[1], cache_k.shape[2]
    n_pages = page_table.shape[1]
    W_b = H_B * D_HB
    C3 = 3 * H_A * DK_A
    alpha = (2.0 * depth) ** 0.25
    tp = pb * seq
    ts = db * SAMPLE_PAD

    xs_pad = jnp.pad(x_sample, ((0, 0), (0, SAMPLE_PAD - dseq), (0, 0)))
    x = jnp.concatenate([x_prompt.reshape(tp, D), xs_pad.reshape(ts, D)], axis=0)
    ck = jnp.transpose(cache_k, (0, 1, 3, 4, 2)).reshape(depth * n_phys, W_b, page)
    cv = jnp.transpose(cache_v, (0, 1, 3, 4, 2)).reshape(depth * n_phys, W_b, page)
    pt_flat = page_table.reshape(-1).astype(jnp.int32)
    buf0 = jnp.zeros((pb, CONV_W - 1, C3), F32)
    s00 = jnp.zeros((pb, H_A, DK_A, DV_A), F32)
    ffn_wg, ffn_wu, ffn_wd = _bf(ffn_w_gate), _bf(ffn_w_up), _bf(ffn_w_down)
    moe_wg, moe_wu, moe_wd = _bf(moe_w_gate), _bf(moe_w_up), _bf(moe_w_down)

    outs = {n: [] for n in ("sp", "cp", "ks", "vs", "ss", "cs")}
    kt_all = vt_all = None
    for li in range(depth):
        w_main, wkv_t = pack_w_in(w_in[li])
        qkv, z, gla, glb, bd, qb = in_projection_main(x, w_main)
        kt_all, vt_all, kt16, vt16 = in_projection_kv_t(x, wkv_t, pb, seq, li, depth, kt_all, vt_all)
        kb_s, vb_s = in_projection_kv_rows(x, wkv_t, tp, ts)
        head_par = jnp.zeros((2, LANE), F32)
        head_par = head_par.at[0, H_A:2 * H_A].set(-jnp.exp(a_log[li].astype(F32)))
        head_par = head_par.at[1, H_A:2 * H_A].set(dt_bias[li].astype(F32))
        norm_w = o_norm_w[li].reshape(1, DV_A).astype(F32)

        qp, kp_, vp_, cbuf_p = conv_prep(qkv, conv_w[li], buf0, 0, pb, seq, 512, seq)
        qs, ks_, vs_, cbuf_s = conv_prep(qkv, conv_w[li], state_conv[li], tp, db, SAMPLE_PAD,
                                         SAMPLE_PAD, dseq, G=16)
        oa_p, s_p = delta_rule_prompt(qp, kp_, vp_, bd, z, head_par, norm_w, s00, pb, seq,
                                      DELTA_CHUNK)
        oa_s, s_s = delta_rule_short(qs, ks_, vs_, bd, z, head_par, norm_w, state_delta[li],
                                     tp, db, SAMPLE_PAD, dseq)
        ob_p = sb_attention_prompt(qb, kt16, vt16, sb_bias[li].astype(F32), pb, seq)
        ob_s = sb_attention_sample(qb, kb_s, vb_s, sb_bias[li].astype(F32).reshape(H_B, 1), ck, cv,
                                   pt_flat, li * n_phys, tp, db, n_pages, dseq)
        oa = jnp.concatenate([oa_p, oa_s], axis=0)
        ob = jnp.concatenate([ob_p, _bf(ob_s)], axis=0)
        ln1 = jnp.stack([ln1_g[li], ln1_b[li]]).astype(F32)
        ln2 = jnp.stack([ln2_g[li], ln2_b[li]]).astype(F32)
        x1 = merge_out_ln(x, oa, ob, gla, glb, _bf(w_proj_a[li]), _bf(w_proj_b[li]), _bf(w_out[li]),
                          ln1, alpha)
        j = li // 2
        if li % 2 == 0:
            x = ffn_ln(x1, ffn_wg, ffn_wu, ffn_wd, j, ln2, alpha)
        else:
            gate = router_gate(x1, router_w[j].astype(F32))
            x = moe_ln(x1, gate, moe_wg, moe_wu, moe_wd, j, ln2, alpha)

        outs["sp"].append(s_p)
        outs["cp"].append(cbuf_p)
        outs["ks"].append(kb_s.reshape(db, SAMPLE_PAD, H_B, D_HB)[:, :dseq])
        outs["vs"].append(vb_s.reshape(db, SAMPLE_PAD, H_B, D_HB)[:, :dseq])
        outs["ss"].append(s_s)
        outs["cs"].append(cbuf_s)

    y_prompt = x[:tp].reshape(pb, seq, D)
    y_sample = x[tp:].reshape(db, SAMPLE_PAD, D)[:, :dseq]
    st = {n: jnp.stack(v) for n, v in outs.items()}
    st["kp"] = jnp.transpose(kt_all.reshape(depth, pb, H_B, D_HB, seq), (0, 1, 4, 2, 3))
    st["vp"] = jnp.transpose(vt_all.reshape(depth, pb, H_B, D_HB, seq), (0, 1, 4, 2, 3))
    return (y_prompt, y_sample, st["kp"], st["vp"], st["sp"], st["cp"],
            st["ks"], st["vs"], st["ss"], st["cs"])
```

```python
import functools
import math

import jax
import jax.numpy as jnp
from jax import lax
from jax.experimental import pallas as pl
from jax.experimental.pallas import tpu as pltpu

F32 = jnp.float32
BF16 = jnp.bfloat16

H_A = 8
DK_A = 64
DV_A = 64
H_B = 8
D_HB = 64
CONV_W = 4
DELTA_CHUNK = 64
LN_EPS = 1e-5
NORM_EPS = 1e-6
SAMPLE_PAD = 8
LANE = 128
VMEM_LIMIT = 56 * 1024 * 1024
LOG2E = 1.4426950408889634
SB_TQ = 512
SB_TK = 256


def _bf(x):
    return x.astype(BF16)


def _dot(a, b):
    return jnp.dot(a, b, preferred_element_type=F32)


def _dot_nt(a, b):
    return lax.dot_general(a, b, (((1,), (1,)), ((), ())), preferred_element_type=F32)


def _dot_tn(a, b):
    return lax.dot_general(a, b, (((0,), (0,)), ((), ())), preferred_element_type=F32)


def _split2(x):
    hi = _bf(x)
    lo = _bf(x - hi.astype(F32))
    return hi, lo


def _split3(x):
    hi = _bf(x)
    r = x - hi.astype(F32)
    mid = _bf(r)
    lo = _bf(r - mid.astype(F32))
    return hi, mid, lo


def _dot3s(a, b):
    return _dot(a[0], b[0]) + (_dot(a[0], b[1]) + _dot(a[1], b[0]))


def _dot3(a, b):
    return _dot3s(_split2(a), _split2(b))


def _silu(x):
    return x * (1.0 / (1.0 + jnp.exp(-x)))


def _sigmoid(x):
    return 1.0 / (1.0 + jnp.exp(-x))


def _softplus(x):
    return jnp.maximum(x, 0.0) + jnp.log1p(jnp.exp(-jnp.abs(x)))


def _layer_norm(x, g, b):
    mu = jnp.mean(x, axis=-1, keepdims=True)
    xc = x - mu
    var = jnp.mean(xc * xc, axis=-1, keepdims=True)
    return xc * lax.rsqrt(var + LN_EPS) * g + b


def _cparams(sem, vmem=VMEM_LIMIT):
    return pltpu.CompilerParams(dimension_semantics=sem, vmem_limit_bytes=vmem)


_PROJ_GROUPS = (("qkv", 1536, F32), ("z", 512, F32), ("gla", 1024, F32), ("glb", 1024, F32),
                ("bd", 128, F32), ("qb", 512, F32))
_PROJ_MAIN = sum(w for _, w, _ in _PROJ_GROUPS)
_W_KV = 2 * H_B * D_HB


def _proj_main_kernel(x_ref, w_ref, *o_refs):
    x = _bf(x_ref[...])
    off = 0
    for (_, width, dt), o_ref in zip(_PROJ_GROUPS, o_refs):
        o_ref[...] = _dot(x, w_ref[:, off:off + width]).astype(dt)
        off += width


def in_projection_main(x, w_main, tm=256):
    T, D = x.shape
    row = lambda i: (i, 0)
    return pl.pallas_call(
        _proj_main_kernel,
        grid=(T // tm,),
        in_specs=[pl.BlockSpec((tm, D), row), pl.BlockSpec((D, _PROJ_MAIN), lambda i: (0, 0))],
        out_specs=[pl.BlockSpec((tm, w), row) for _, w, _ in _PROJ_GROUPS],
        out_shape=[jax.ShapeDtypeStruct((T, w), dt) for _, w, dt in _PROJ_GROUPS],
        compiler_params=_cparams(("parallel",)),
        name="in_projection_main",
    )(x, w_main)


def _proj_kv_rows_kernel(x_ref, wkv_ref, k32, v32):
    x = _bf(x_ref[...])
    wb = H_B * D_HB
    k32[...] = _dot_nt(x, wkv_ref[:wb, :])
    v32[...] = _dot_nt(x, wkv_ref[wb:, :])


def in_projection_kv_rows(x, wkv_t, row0, nrows, tm=256):
    D = x.shape[1]
    W = H_B * D_HB
    rb0 = row0 // tm
    return pl.pallas_call(
        _proj_kv_rows_kernel,
        grid=(nrows // tm,),
        in_specs=[pl.BlockSpec((tm, D), lambda i: (rb0 + i, 0)),
                  pl.BlockSpec((_W_KV, D), lambda i: (0, 0))],
        out_specs=[pl.BlockSpec((tm, W), lambda i: (i, 0))] * 2,
        out_shape=[jax.ShapeDtypeStruct((nrows, W), F32)] * 2,
        compiler_params=_cparams(("parallel",)),
        name="in_projection_kv_rows",
    )(x, wkv_t)


def _proj_kv_t_kernel(x_ref, wkv_ref, *refs):
    k32, v32, kb16, vb16 = refs[-4:]
    x = _bf(x_ref[...])
    wb = H_B * D_HB
    kt = _dot_nt(wkv_ref[:wb, :], x)
    vt = _dot_nt(wkv_ref[wb:, :], x)
    k32[0, 0] = kt
    v32[0, 0] = vt
    kb16[0, 0] = _bf(kt)
    vb16[0, 0] = _bf(vt)


def in_projection_kv_t(x, wkv_t, nseq, seqlen, layer, depth, k_all=None, v_all=None, tm=SB_TK):
    D = x.shape[1]
    W = H_B * D_HB
    nt = seqlen // tm
    t32 = pl.BlockSpec((1, 1, W, tm), lambda i: (layer, i // nt, 0, i % nt))
    t16 = pl.BlockSpec((1, 1, W, tm), lambda i: (i // nt, i % nt, 0, 0))
    all_shape = jax.ShapeDtypeStruct((depth, nseq, W, seqlen), F32)
    in_specs = [pl.BlockSpec((tm, D), lambda i: (i, 0)),
                pl.BlockSpec((_W_KV, D), lambda i: (0, 0))]
    args = [x, wkv_t]
    aliases = {}
    if k_all is not None:
        in_specs += [pl.BlockSpec(memory_space=pl.ANY)] * 2
        args += [k_all, v_all]
        aliases = {2: 0, 3: 1}
    return pl.pallas_call(
        _proj_kv_t_kernel,
        grid=(nseq * nt,),
        in_specs=in_specs,
        out_specs=[t32, t32, t16, t16],
        out_shape=[all_shape, all_shape] + [jax.ShapeDtypeStruct((nseq, nt, W, tm), BF16)] * 2,
        input_output_aliases=aliases,
        compiler_params=_cparams(("parallel",)),
        name="in_projection_kv_t",
    )(*args)


def pack_w_in(w_in):
    c = 3 * H_A * DK_A
    wv = H_A * DV_A
    wb = H_B * D_HB
    D = w_in.shape[0]
    i = 0
    qkv = w_in[:, i:i + c]; i += c
    z = w_in[:, i:i + wv]; i += wv
    bd = w_in[:, i:i + 2 * H_A]; i += 2 * H_A
    qb = w_in[:, i:i + wb]; i += wb
    kv = w_in[:, i:i + 2 * wb]; i += 2 * wb
    gla = w_in[:, i:i + D]; i += D
    glb = w_in[:, i:i + D]; i += D
    bd = jnp.pad(bd, ((0, 0), (0, LANE - 2 * H_A)))
    main = _bf(jnp.concatenate([qkv, z, gla, glb, bd, qb], axis=1))
    return main, _bf(kv.T)


def _prep_kernel(x_ref, w_ref, buf_ref, hsum_ref, q_ref, k_ref, v_ref, nb_ref, xp_ref,
                 *, tt, valid_last, G):
    t = pl.program_id(1)
    nt = pl.num_programs(1)
    halo = CONV_W - 1
    span = tt + 8
    w = w_ref[...]
    ys = []
    for g in range(G):
        base = g * span + 8

        @pl.when(t == 0)
        def _():
            xp_ref[base - halo:base, :] = buf_ref[g]

        if G == 1:
            @pl.when(t > 0)
            def _():
                xp_ref[base - halo:base, :] = xp_ref[base + tt - halo:base + tt, :]

        xp_ref[base:base + tt, :] = x_ref[g * tt:(g + 1) * tt, :]
        y = xp_ref[base - halo:base - halo + tt, :] * w[0:1, :]
        for i in range(1, CONV_W):
            y = y + xp_ref[base - halo + i:base - halo + i + tt, :] * w[i:i + 1, :]
        ys.append(y)
    y = _silu(ys[0] if G == 1 else jnp.concatenate(ys, axis=0))
    wq = H_A * DK_A
    q = y[:, :wq]
    k = y[:, wq:2 * wq]
    v_ref[...] = y[:, 2 * wq:]
    hs = hsum_ref[...]

    def l2n(a):
        hi, lo = _split2(a * a)
        ss = _dot(hi, hs) + _dot(lo, hs)
        return a * lax.rsqrt(ss + NORM_EPS)

    q_ref[...] = l2n(q) * (DK_A ** -0.5)
    k_ref[...] = l2n(k)

    @pl.when(t == nt - 1)
    def _():
        for g in range(G):
            base = g * span + 8
            nb_ref[g] = xp_ref[base + valid_last - halo:base + valid_last, :]


def conv_prep(qkv, conv_w, buf, row0, nseq, seqlen, tt, valid_len, G=1):
    C = qkv.shape[1]
    T = nseq * seqlen
    W = C // 3
    nt = seqlen // tt
    assert G == 1 or nt == 1
    R = G * tt
    rb0 = row0 // R
    valid_last = valid_len - (nt - 1) * tt
    head = jnp.arange(W, dtype=jnp.int32) // DK_A
    hsum = (head[:, None] == head[None, :]).astype(BF16)
    row_map = lambda b, t: (b * nt + t, 0)
    kern = functools.partial(_prep_kernel, tt=tt, valid_last=valid_last, G=G)
    return pl.pallas_call(
        kern,
        grid=(nseq // G, nt),
        in_specs=[pl.BlockSpec((R, C), lambda b, t: (rb0 + b * nt + t, 0)),
                  pl.BlockSpec((CONV_W, C), lambda b, t: (0, 0)),
                  pl.BlockSpec((G, CONV_W - 1, C), lambda b, t: (b, 0, 0)),
                  pl.BlockSpec((W, W), lambda b, t: (0, 0))],
        out_specs=[pl.BlockSpec((R, W), row_map),
                   pl.BlockSpec((R, W), row_map),
                   pl.BlockSpec((R, W), row_map),
                   pl.BlockSpec((G, CONV_W - 1, C), lambda b, t: (b, 0, 0))],
        out_shape=[jax.ShapeDtypeStruct((T, W), F32)] * 3
                  + [jax.ShapeDtypeStruct((nseq, CONV_W - 1, C), F32)],
        scratch_shapes=[pltpu.VMEM((G * (tt + 8), C), F32)],
        compiler_params=_cparams(("parallel", "arbitrary")),
        name="conv_prep",
    )(qkv, conv_w, buf, hsum)


def _delta_gates(bd, hp, C, valid_len, tri_incl):
    beta_all = _sigmoid(bd)
    g_all = hp[0:1, :] * _softplus(bd + hp[1:2, :])
    if valid_len < C:
        pos = lax.broadcasted_iota(jnp.int32, bd.shape, 0) % C
        live = pos < valid_len
        beta_all = jnp.where(live, beta_all, 0.0)
        g_all = jnp.where(live, g_all, 0.0)
    gcs = []
    for g in range(bd.shape[0] // C):
        gh, gm, gl = _split3(g_all[g * C:(g + 1) * C, :])
        gcs.append(_dot(tri_incl, gh) + (_dot(tri_incl, gm) + _dot(tri_incl, gl)))
    return beta_all, gcs


def _delta_intra(qs, ks, vs, betas, gcs, C, nstage):
    n = len(qs)
    row = lax.broadcasted_iota(jnp.int32, (C, C), 0)
    col = lax.broadcasted_iota(jnp.int32, (C, C), 1)
    incl = row >= col
    strict = row > col
    eye = row == col
    ones_cc = jnp.ones((C, C), BF16)
    rng = range(n)
    a_mats = [jnp.broadcast_to(gcs[i], (C, C)) for i in rng]
    dsp = [_split3(jnp.where(eye, a_mats[i], 0.0)) for i in rng]
    b_mats = [_dot_nt(ones_cc, dsp[i][0]) + (_dot_nt(ones_cc, dsp[i][1]) + _dot_nt(ones_cc, dsp[i][2]))
              for i in rng]
    decays = [jnp.where(incl, jnp.exp(jnp.minimum(a_mats[i] - b_mats[i], 0.0)), 0.0) for i in rng]
    kbs = [ks[i] * betas[i] for i in rng]
    kks = [_dot_nt(_bf(kbs[i]), _bf(ks[i])) for i in rng]
    qks = [_dot_nt(_bf(qs[i]), _bf(ks[i])) for i in rng]
    egs = [jnp.exp(gcs[i]) for i in rng]
    ps = [_split2(-jnp.where(strict, kks[i] * decays[i], 0.0)) for i in rng]
    rs = [jnp.concatenate([vs[i] * betas[i], kbs[i] * egs[i]], axis=-1) for i in rng]
    for s in range(min(2, nstage)):
        if s > 0:
            pn = [_dot3s(ps[i], ps[i]) for i in rng]
            ps = [_split2(pn[i]) for i in rng]
        rsp = [_split2(rs[i]) for i in rng]
        dr = [_dot3s(ps[i], rsp[i]) for i in rng]
        rs = [rs[i] + dr[i] for i in rng]
    pb = [ps[i][0] for i in rng]
    for _ in range(nstage - 2):
        pb = [_bf(_dot(pb[i], pb[i])) for i in rng]
        dr = [_dot(pb[i], _bf(rs[i])) for i in rng]
        rs = [rs[i] + dr[i] for i in rng]
    us = [rs[i][:, :DV_A] for i in rng]
    ws = [rs[i][:, DV_A:] for i in rng]
    qkm = [jnp.where(incl, qks[i] * decays[i], 0.0) for i in rng]
    kdecs = [ks[i] * jnp.exp(gcs[i][C - 1:C, :] - gcs[i]) for i in rng]
    qgs = [qs[i] * egs[i] for i in rng]
    return us, ws, qkm, kdecs, qgs


def _delta_state(us, ws, qkm, kdecs, qgs, egl, states):
    rng = range(len(us))
    sb = [_bf(states[i]) for i in rng]
    ws_s = [_dot(ws[i], sb[i]) for i in rng]
    qs_s = [_dot(qgs[i], sb[i]) for i in rng]
    vn = [_bf(us[i] - ws_s[i]) for i in rng]
    os_ = [qs_s[i] + _dot(qkm[i], vn[i]) for i in rng]
    new = [states[i] * egl[i] + _dot_tn(kdecs[i], vn[i]) for i in rng]
    return os_, new


def _gated_rms(o, nw, z):
    rn = o * lax.rsqrt(jnp.mean(o * o, axis=-1, keepdims=True) + NORM_EPS)
    return rn * nw * _silu(z)


def _delta_intra_kernel(q_ref, k_ref, v_ref, bd_ref, hp_ref,
                        u_ref, w_ref, qk_ref, kd_ref, qg_ref, egl_ref, *, C, G, nstage):
    row = lax.broadcasted_iota(jnp.int32, (C, C), 0)
    col = lax.broadcasted_iota(jnp.int32, (C, C), 1)
    tri_incl = jnp.where(row >= col, 1.0, 0.0).astype(BF16)
    beta_all, gcs = _delta_gates(bd_ref[...], hp_ref[...], C, C, tri_incl)
    qs, ks, vs, betas, gch = [], [], [], [], []
    for g in range(G):
        rows = slice(g * C, (g + 1) * C)
        for h in range(H_A):
            sl = slice(h * DK_A, (h + 1) * DK_A)
            qs.append(q_ref[rows, sl])
            ks.append(k_ref[rows, sl])
            vs.append(v_ref[rows, sl])
            betas.append(beta_all[rows, h:h + 1])
            gch.append(gcs[g][:, H_A + h:H_A + h + 1])
    us, ws, qkm, kdecs, qgs = _delta_intra(qs, ks, vs, betas, gch, C, nstage)
    for g in range(G):
        rows = slice(g * C, (g + 1) * C)
        pr = slice(g * H_A, (g + 1) * H_A)
        u_ref[rows, :] = jnp.concatenate(us[pr], axis=-1)
        w_ref[rows, :] = _bf(jnp.concatenate(ws[pr], axis=-1))
        qk_ref[rows, :] = _bf(jnp.concatenate(qkm[pr], axis=-1))
        kd_ref[rows, :] = _bf(jnp.concatenate(kdecs[pr], axis=-1))
        qg_ref[rows, :] = _bf(jnp.concatenate(qgs[pr], axis=-1))
        egl_ref[0, g] = jnp.exp(gcs[g][C - 1:C, :])


def _delta_seq_kernel(u_ref, w_ref, qk_ref, kd_ref, qg_ref, egl_ref, nw_ref, s0_ref, *rest, nb):
    z_refs = rest[:nb]
    o_ref, sf_ref, s_scr = rest[nb:]
    c = pl.program_id(0)
    nc = pl.num_programs(0)

    @pl.when(c == 0)
    def _():
        s_scr[...] = s0_ref[...]

    nw = nw_ref[...]
    us, ws, qkm, kdecs, qgs, egl, states = [], [], [], [], [], [], []
    for b in range(nb):
        er = egl_ref[b, 0]
        for h in range(H_A):
            sl = slice(h * DK_A, (h + 1) * DK_A)
            us.append(u_ref[b, :, sl])
            ws.append(w_ref[b, :, sl])
            qkm.append(qk_ref[b, :, sl])
            kdecs.append(kd_ref[b, :, sl])
            qgs.append(qg_ref[b, :, sl])
            egl.append(er[:, H_A + h:H_A + h + 1])
            states.append(s_scr[b, h])
    os_, new = _delta_state(us, ws, qkm, kdecs, qgs, egl, states)
    for b in range(nb):
        outs = []
        for h in range(H_A):
            i = b * H_A + h
            s_scr[b, h] = new[i]
            outs.append(_gated_rms(os_[i], nw, z_refs[b][:, h * DV_A:(h + 1) * DV_A]))
        o_ref[b] = _bf(jnp.concatenate(outs, axis=-1))

    @pl.when(c == nc - 1)
    def _():
        sf_ref[...] = s_scr[...]


def delta_rule_prompt(q, k, v, bd, z, head_par, norm_w, s0, nseq, seqlen, C, G=4):
    T, W = q.shape
    nc = seqlen // C
    nstage = max(1, int(math.log2(C)))
    R = G * C
    ncb = nc // G
    row = lambda i: (i, 0)
    intra = pl.pallas_call(
        functools.partial(_delta_intra_kernel, C=C, G=G, nstage=nstage),
        grid=(T // R,),
        in_specs=[pl.BlockSpec((R, W), row)] * 3
                 + [pl.BlockSpec((R, LANE), row), pl.BlockSpec((2, LANE), lambda i: (0, 0))],
        out_specs=[pl.BlockSpec((R, W), row)] * 5
                  + [pl.BlockSpec((1, G, 1, LANE), lambda i: (i // ncb, i % ncb, 0, 0))],
        out_shape=[jax.ShapeDtypeStruct((T, W), F32)] + [jax.ShapeDtypeStruct((T, W), BF16)] * 4
                  + [jax.ShapeDtypeStruct((nseq, nc, 1, LANE), F32)],
        compiler_params=_cparams(("parallel",)),
        name="delta_intra",
    )
    u, w, qk, kd, qg, egl = intra(q, k, v, bd, head_par)
    r3 = lambda a: a.reshape(nseq, seqlen, W)
    blk = pl.BlockSpec((nseq, C, W), lambda c: (0, c, 0))
    st = pl.BlockSpec((nseq, H_A, DK_A, DV_A), lambda c: (0, 0, 0, 0))
    o, sf = pl.pallas_call(
        functools.partial(_delta_seq_kernel, nb=nseq),
        grid=(nc,),
        in_specs=[blk] * 5 + [pl.BlockSpec((nseq, 1, 1, LANE), lambda c: (0, c, 0, 0)),
                              pl.BlockSpec((1, DV_A), lambda c: (0, 0)), st]
                 + [pl.BlockSpec((C, W), functools.partial(lambda c, b: (b * nc + c, 0), b=b))
                    for b in range(nseq)],
        out_specs=[blk, st],
        out_shape=[jax.ShapeDtypeStruct((nseq, seqlen, W), BF16),
                   jax.ShapeDtypeStruct((nseq, H_A, DK_A, DV_A), F32)],
        scratch_shapes=[pltpu.VMEM((nseq, H_A, DK_A, DV_A), F32)],
        compiler_params=_cparams(("arbitrary",)),
        name="delta_seq",
    )(r3(u), r3(w), r3(qk), r3(kd), r3(qg), egl, norm_w, s0, *([z] * nseq))
    return o.reshape(T, W), sf


def _delta_short_kernel(q_ref, k_ref, v_ref, bd_ref, z_ref, hp_ref, nw_ref, s0_ref,
                        o_ref, sf_ref, *, C, G, valid_len, nstage):
    row = lax.broadcasted_iota(jnp.int32, (C, C), 0)
    col = lax.broadcasted_iota(jnp.int32, (C, C), 1)
    tri_incl = jnp.where(row >= col, 1.0, 0.0).astype(BF16)
    beta_all, gcs = _delta_gates(bd_ref[...], hp_ref[...], C, valid_len, tri_incl)
    qs, ks, vs, betas, gch, egl, states = [], [], [], [], [], [], []
    for g in range(G):
        rows = slice(g * C, (g + 1) * C)
        for h in range(H_A):
            sl = slice(h * DK_A, (h + 1) * DK_A)
            qs.append(q_ref[rows, sl])
            ks.append(k_ref[rows, sl])
            vs.append(v_ref[rows, sl])
            betas.append(beta_all[rows, h:h + 1])
            gc = gcs[g][:, H_A + h:H_A + h + 1]
            gch.append(gc)
            egl.append(jnp.exp(gc[C - 1:C, :]))
            states.append(s0_ref[g, h])
    us, ws, qkm, kdecs, qgs = _delta_intra(qs, ks, vs, betas, gch, C, nstage)
    os_, new = _delta_state(us, [_bf(a) for a in ws], [_bf(a) for a in qkm],
                            [_bf(a) for a in kdecs], [_bf(a) for a in qgs], egl, states)
    nw = nw_ref[...]
    for g in range(G):
        rows = slice(g * C, (g + 1) * C)
        outs = []
        for h in range(H_A):
            i = g * H_A + h
            sf_ref[g, h] = new[i]
            outs.append(_gated_rms(os_[i], nw, z_ref[rows, h * DV_A:(h + 1) * DV_A]))
        o_ref[rows, :] = _bf(jnp.concatenate(outs, axis=-1))


def delta_rule_short(q, k, v, bd, z, head_par, norm_w, s0, row0, nseq, C, valid_len, G=8):
    T, W = q.shape
    nstage = max(1, int(math.log2(C)))
    R = G * C
    rb0 = row0 // R
    row = lambda i: (i, 0)
    off = lambda i: (rb0 + i, 0)
    st = pl.BlockSpec((G, H_A, DK_A, DV_A), lambda i: (i, 0, 0, 0))
    return pl.pallas_call(
        functools.partial(_delta_short_kernel, C=C, G=G, valid_len=valid_len, nstage=nstage),
        grid=(nseq // G,),
        in_specs=[pl.BlockSpec((R, W), row)] * 3
                 + [pl.BlockSpec((R, LANE), off), pl.BlockSpec((R, W), off),
                    pl.BlockSpec((2, LANE), lambda i: (0, 0)),
                    pl.BlockSpec((1, DV_A), lambda i: (0, 0)), st],
        out_specs=[pl.BlockSpec((R, W), row), st],
        out_shape=[jax.ShapeDtypeStruct((T, W), BF16),
                   jax.ShapeDtypeStruct((nseq, H_A, DK_A, DV_A), F32)],
        compiler_params=_cparams(("parallel",)),
        name="delta_short",
    )(q, k, v, bd, z, head_par, norm_w, s0)


def _softplus2(z2):
    return jnp.maximum(jnp.log(1.0 + jnp.exp2(jnp.minimum(z2, 64.0))) * LOG2E, z2)


def _sb_prompt_kernel(bias_ref, q_ref, kt_ref, vt_ref, o_ref, *, tq, tk):
    hp = pl.program_id(1)
    i = pl.program_id(2)
    lane = lax.broadcasted_iota(jnp.int32, (tq, LANE), 1)
    first = lane < D_HB
    q = _bf(q_ref[...])
    zero = jnp.zeros_like(q)
    qm = (jnp.where(first, q, zero), jnp.where(first, zero, q))
    scale2 = (D_HB ** -0.5) * LOG2E
    bias2 = (bias_ref[2 * hp] * LOG2E, bias_ref[2 * hp + 1] * LOG2E)
    r = lax.broadcasted_iota(jnp.int32, (tk, tk + LANE), 0)
    c = lax.broadcasted_iota(jnp.int32, (tk, tk + LANE), 1)
    suffix = jnp.where(jnp.logical_or(c >= tk, r > c), -1.0, 0.0).astype(BF16)
    nrep = tk // LANE
    heads = (0, 1)

    def tile(j, carry, acc, masked, row0=0):
        kt = kt_ref[0, j]
        vt = vt_ref[0, j]
        live = slice(row0, tq)
        rows = tq - row0
        zs = [_dot(qm[h][live], kt) * scale2 + bias2[h] for h in heads]
        sps = [_softplus2(zs[h]) for h in heads]
        if masked:
            qpos = i * tq + row0 + lax.broadcasted_iota(jnp.int32, (rows, tk), 0)
            kpos = j * tk + lax.broadcasted_iota(jnp.int32, (rows, tk), 1)
            mask = kpos < qpos
            spm = [jnp.where(mask, sps[h], 0.0) for h in heads]
        else:
            spm = sps
        cm = [_dot(_bf(spm[h]), suffix) for h in heads]
        ex = [zs[h] - sps[h] + (cm[h][:, :tk] + jnp.concatenate([carry[h][live]] * nrep, axis=1))
              for h in heads]
        av = [jnp.exp2(ex[h]) for h in heads]
        if masked:
            av = [jnp.where(mask, av[h], 0.0) for h in heads]
        pv = [_dot_nt(_bf(av[h]), vt) for h in heads]
        new_c = [carry[h][live] + cm[h][:, tk:] for h in heads]
        head0 = lax.broadcasted_iota(jnp.int32, (rows, LANE), 1) < D_HB
        new_a = acc[live] + jnp.where(head0, pv[0], pv[1])
        if row0:
            new_c = [jnp.concatenate([carry[h][:row0], new_c[h]], axis=0) for h in heads]
            new_a = jnp.concatenate([acc[:row0], new_a], axis=0)
        return tuple(new_c), new_a

    zc = jnp.zeros((tq, LANE), F32)
    carry, acc = (zc, zc), zc
    nd = tq // tk
    j0 = i * nd
    for dj in reversed(range(nd)):
        carry, acc = tile(j0 + dj, carry, acc, True, row0=dj * tk)

    def body(jj, st):
        carry, acc = (st[0], st[1]), st[2]
        for d in range(nd):
            carry, acc = tile(j0 - 1 - jj * nd - d, carry, acc, False)
        return carry[0], carry[1], acc

    _, _, acc = lax.fori_loop(0, i, body, (carry[0], carry[1], acc))
    o_ref[...] = _bf(acc)


def sb_attention_prompt(qb, kt16, vt16, sb_bias, nseq, seqlen, tq=SB_TQ):
    W = qb.shape[1]
    nt, tk = kt16.shape[1], kt16.shape[3]
    nq = seqlen // tq
    nhp = W // LANE
    kern = functools.partial(_sb_prompt_kernel, tq=tq, tk=tk)
    kv_spec = pl.BlockSpec((1, nt, LANE, tk), lambda b, hp, i: (b, 0, hp, 0))
    return pl.pallas_call(
        kern,
        grid=(nseq, nhp, nq),
        in_specs=[pl.BlockSpec(memory_space=pltpu.SMEM),
                  pl.BlockSpec((tq, LANE), lambda b, hp, i: (b * nq + i, hp)),
                  kv_spec, kv_spec],
        out_specs=pl.BlockSpec((tq, LANE), lambda b, hp, i: (b * nq + i, hp)),
        out_shape=jax.ShapeDtypeStruct((nseq * seqlen, W), BF16),
        compiler_params=_cparams(("parallel", "parallel", "arbitrary")),
        name="sb_attention_prompt",
    )(sb_bias, qb, kt16, vt16)


def _strict_lower_ones(n):
    r = lax.broadcasted_iota(jnp.int32, (n, n), 0)
    c = lax.broadcasted_iota(jnp.int32, (n, n), 1)
    return jnp.where(r > c, 1.0, 0.0).astype(BF16)


def _sb_sample_kernel(pt_ref, q_ref, ko_ref, vo_ref, bias_ref, *rest, n_pages, n_tok, page):
    k_refs = rest[:n_pages]
    v_refs = rest[n_pages:2 * n_pages]
    o_ref = rest[2 * n_pages]
    W = H_B * D_HB
    R = n_tok * H_B
    hrow = lax.broadcasted_iota(jnp.int32, (H_B, W), 0)
    hlane = lax.broadcasted_iota(jnp.int32, (H_B, W), 1) // D_HB
    hmask = hrow == hlane
    q = q_ref[...]
    qm = _bf(jnp.concatenate(
        [jnp.where(hmask, jnp.broadcast_to(q[t:t + 1, :], (H_B, W)), 0.0) for t in range(n_tok)],
        axis=0))
    bias2 = jnp.concatenate([bias_ref[...]] * n_tok, axis=0) * LOG2E
    scale2 = (D_HB ** -0.5) * LOG2E
    no = ko_ref.shape[0]
    trow = lax.broadcasted_iota(jnp.int32, (R, no), 0) // H_B
    scol = lax.broadcasted_iota(jnp.int32, (R, no), 1)
    own_mask = scol < trow
    order = list(reversed(range(n_pages)))
    zs = [_dot_nt(qm, _bf(ko_ref[...]))] + [_dot(qm, _bf(k_refs[p][0])) for p in order]
    zs = [z * scale2 + bias2 for z in zs]
    sps = [_softplus2(z) for z in zs]
    spb = [_bf(jnp.where(own_mask, sps[0], 0.0))] + [_bf(sp) for sp in sps[1:]]
    r = lax.broadcasted_iota(jnp.int32, (page, page + LANE), 0)
    c = lax.broadcasted_iota(jnp.int32, (page, page + LANE), 1)
    suffix = jnp.where(jnp.logical_or(c >= page, r > c), -1.0, 0.0).astype(BF16)
    own_local = _dot(spb[0], -_strict_lower_ones(no))
    own_total = _dot(spb[0], jnp.full((no, LANE), -1.0, BF16))
    cms = [_dot(sp, suffix) for sp in spb[1:]]
    carry = own_total
    exs = [zs[0] - sps[0] + own_local]
    for t, cm in enumerate(cms):
        exs.append(zs[t + 1] - sps[t + 1] + (cm[:, :page] + carry))
        carry = carry + cm[:, page:]
    avs = [jnp.exp2(ex) for ex in exs]
    acc = _dot(_bf(jnp.where(own_mask, avs[0], 0.0)), _bf(vo_ref[...]))
    for t, p in enumerate(order):
        acc = acc + _dot_nt(_bf(avs[t + 1]), _bf(v_refs[p][0]))
    rows = [jnp.sum(jnp.where(hmask, acc[t * H_B:(t + 1) * H_B, :], 0.0), axis=0, keepdims=True)
            for t in range(n_tok)]
    rows.append(jnp.zeros((q.shape[0] - n_tok, W), F32))
    o_ref[...] = jnp.concatenate(rows, axis=0)


def sb_attention_sample(qb, kb, vb, sb_bias_col, cache_k, cache_v, page_table_flat, layer,
                        row0, nseq, n_pages, n_tok):
    W = qb.shape[1]
    page = cache_k.shape[2]
    rb0 = row0 // SAMPLE_PAD
    own_q = pl.BlockSpec((SAMPLE_PAD, W), lambda b, pt: (rb0 + b, 0))
    own = pl.BlockSpec((SAMPLE_PAD, W), lambda b, pt: (b, 0))

    def page_spec(p):
        return pl.BlockSpec((1, W, page), lambda b, pt: (layer + pt[b * n_pages + p], 0, 0))

    kern = functools.partial(_sb_sample_kernel, n_pages=n_pages, n_tok=n_tok, page=page)
    grid_spec = pltpu.PrefetchScalarGridSpec(
        num_scalar_prefetch=1,
        grid=(nseq,),
        in_specs=[own_q, own, own, pl.BlockSpec((H_B, 1), lambda b, pt: (0, 0))]
                 + [page_spec(p) for p in range(n_pages)] * 2,
        out_specs=pl.BlockSpec((SAMPLE_PAD, W), lambda b, pt: (b, 0)),
    )
    return pl.pallas_call(
        kern,
        grid_spec=grid_spec,
        out_shape=jax.ShapeDtypeStruct((nseq * SAMPLE_PAD, W), F32),
        compiler_params=_cparams(("arbitrary",)),
        name="sb_attention_sample",
    )(page_table_flat, qb, kb, vb, sb_bias_col, *([cache_k] * n_pages), *([cache_v] * n_pages))


def _merge_kernel(x_ref, oa_ref, ob_ref, gla_ref, glb_ref, wa_ref, wb_ref, wo_ref, ln_ref,
                  o_ref, *, alpha):
    pa = _dot(oa_ref[...], wa_ref[...])
    pb = _dot(ob_ref[...], wb_ref[...])
    m = _sigmoid(gla_ref[...]) * pa + _sigmoid(glb_ref[...]) * pb
    y = _dot(_bf(m), wo_ref[...])
    ln = ln_ref[...]
    o_ref[...] = _layer_norm(alpha * x_ref[...] + y, ln[0:1, :], ln[1:2, :])


def merge_out_ln(x, oa, ob, gla, glb, wa, wb, wo, ln, alpha, tm=512):
    T, D = x.shape
    W = oa.shape[1]
    row = lambda i: (i, 0)
    full = lambda i: (0, 0)
    return pl.pallas_call(
        functools.partial(_merge_kernel, alpha=alpha),
        grid=(T // tm,),
        in_specs=[pl.BlockSpec((tm, D), row), pl.BlockSpec((tm, W), row), pl.BlockSpec((tm, W), row),
                  pl.BlockSpec((tm, D), row), pl.BlockSpec((tm, D), row),
                  pl.BlockSpec((W, D), full), pl.BlockSpec((W, D), full), pl.BlockSpec((D, D), full),
                  pl.BlockSpec((2, D), full)],
        out_specs=pl.BlockSpec((tm, D), row),
        out_shape=jax.ShapeDtypeStruct((T, D), F32),
        compiler_params=_cparams(("parallel",)),
        name="merge_out_ln",
    )(x, oa, ob, gla, glb, wa, wb, wo, ln)


def _router_kernel(x_ref, w_ref, g_ref, *, n_exp):
    logits = _dot3(x_ref[...], w_ref[...])
    lane = lax.broadcasted_iota(jnp.int32, logits.shape, 1)
    neg = jnp.float32(-jnp.inf)
    l1 = jnp.where(lane < n_exp, logits, neg)
    m1 = jnp.max(l1, axis=-1, keepdims=True)
    i1 = jnp.min(jnp.where(l1 == m1, lane, LANE), axis=-1, keepdims=True)
    l2 = jnp.where(lane == i1, neg, l1)
    m2 = jnp.max(l2, axis=-1, keepdims=True)
    i2 = jnp.min(jnp.where(l2 == m2, lane, LANE), axis=-1, keepdims=True)
    e = jnp.exp(m2 - m1)
    w1 = 1.0 / (1.0 + e)
    g_ref[...] = jnp.where(lane == i1, w1, 0.0) + jnp.where(lane == i2, e * w1, 0.0)


def router_gate(x, router_w, tm=512):
    T, D = x.shape
    n_exp = router_w.shape[1]
    wpad = jnp.pad(router_w, ((0, 0), (0, LANE - n_exp)))
    return pl.pallas_call(
        functools.partial(_router_kernel, n_exp=n_exp),
        grid=(T // tm,),
        in_specs=[pl.BlockSpec((tm, D), lambda i: (i, 0)), pl.BlockSpec((D, LANE), lambda i: (0, 0))],
        out_specs=pl.BlockSpec((tm, LANE), lambda i: (i, 0)),
        out_shape=jax.ShapeDtypeStruct((T, LANE), F32),
        compiler_params=_cparams(("parallel",)),
        name="router_gate",
    )(x, wpad)


def _ffn_kernel(x_ref, wg_ref, wu_ref, wd_ref, ln_ref, o_ref, acc_ref, *, alpha):
    f = pl.program_id(1)

    @pl.when(f == 0)
    def _():
        acc_ref[...] = jnp.zeros_like(acc_ref)

    x = x_ref[...]
    xb = _bf(x)
    h = _silu(_dot(xb, wg_ref[0])) * _dot(xb, wu_ref[0])
    acc_ref[...] += _dot(_bf(h), wd_ref[0])

    @pl.when(f == pl.num_programs(1) - 1)
    def _():
        ln = ln_ref[...]
        o_ref[...] = _layer_norm(alpha * x + acc_ref[...], ln[0:1, :], ln[1:2, :])


def ffn_ln(x, wg, wu, wd, j, ln, alpha, tm=512):
    T, D = x.shape
    F = wg.shape[2]
    tf = next(c for c in (1408, 1024, 512, 256, 128) if F % c == 0)
    row = lambda i, f: (i, 0)
    return pl.pallas_call(
        functools.partial(_ffn_kernel, alpha=alpha),
        grid=(T // tm, F // tf),
        in_specs=[pl.BlockSpec((tm, D), row),
                  pl.BlockSpec((1, D, tf), lambda i, f: (j, 0, f)),
                  pl.BlockSpec((1, D, tf), lambda i, f: (j, 0, f)),
                  pl.BlockSpec((1, tf, D), lambda i, f: (j, f, 0)),
                  pl.BlockSpec((2, D), lambda i, f: (0, 0))],
        out_specs=pl.BlockSpec((tm, D), row),
        out_shape=jax.ShapeDtypeStruct((T, D), F32),
        scratch_shapes=[pltpu.VMEM((tm, D), F32)],
        compiler_params=_cparams(("parallel", "arbitrary")),
        name="ffn_ln",
    )(x, wg, wu, wd, ln)


MOE_TOKENS = 1024
MOE_ROWS = 256
MOE_TAIL_ROWS = 128


def _moe_kernel(cnt_ref, x_ref, rkt_ref, rk_ref, gate_ref, wg_ref, wu_ref, wd_ref, ln_ref, o_ref,
                xb_ref, xs_ref, ya_ref, acc_ref, *, alpha, n_exp):
    i = pl.program_id(0)
    e = pl.program_id(1)
    f = pl.program_id(2)
    nf = pl.num_programs(2)
    TB, D = x_ref.shape
    cnt = cnt_ref[i * n_exp + e]
    n_full = cnt // MOE_ROWS
    rem = cnt - n_full * MOE_ROWS

    def for_tiles(fn):
        def body(s, carry):
            fn(pl.multiple_of(s * MOE_ROWS, MOE_ROWS), MOE_ROWS)
            return carry

        lax.fori_loop(0, n_full, body, 0)
        r_tail = pl.multiple_of(n_full * MOE_ROWS, MOE_ROWS)

        @pl.when(rem > MOE_TAIL_ROWS)
        def _():
            fn(r_tail, MOE_ROWS)

        @pl.when(jnp.logical_and(rem > 0, rem <= MOE_TAIL_ROWS))
        def _():
            fn(r_tail, MOE_TAIL_ROWS)

    @pl.when(jnp.logical_and(e == 0, f == 0))
    def _():
        acc_ref[...] = jnp.zeros_like(acc_ref)
        xb_ref[...] = _bf(x_ref[...])

    @pl.when(f == 0)
    def _():
        rke = rkt_ref[0, pl.ds(e, 1), :]

        def gather(r0, rows):
            ridx = lax.broadcasted_iota(jnp.int32, (rows, TB), 0).astype(F32)
            onehot = jnp.where(rke - r0.astype(F32) == ridx, 1.0, 0.0).astype(BF16)
            xs_ref[pl.ds(r0, rows), :] = _bf(_dot(onehot, xb_ref[...]))
            ya_ref[pl.ds(r0, rows), :] = jnp.zeros((rows, D), F32)

        for_tiles(gather)

    def expert(r0, rows):
        xsub = xs_ref[pl.ds(r0, rows), :]
        h = _silu(_dot(xsub, wg_ref[0, 0])) * _dot(xsub, wu_ref[0, 0])
        ya_ref[pl.ds(r0, rows), :] += _dot(_bf(h), wd_ref[0, 0])

    for_tiles(expert)

    @pl.when(f == nf - 1)
    def _():
        lane = lax.broadcasted_iota(jnp.int32, (TB, LANE), 1)
        ge = jnp.sum(jnp.where(lane == e, gate_ref[...], 0.0), axis=-1, keepdims=True)
        rkc = jnp.sum(jnp.where(lane == e, rk_ref[...], 0.0), axis=-1, keepdims=True)

        def combine(r0, rows):
            cidx = lax.broadcasted_iota(jnp.int32, (TB, rows), 1).astype(F32)
            onehot_t = jnp.where(rkc - r0.astype(F32) == cidx, 1.0, 0.0).astype(BF16)
            acc_ref[...] += ge * _dot(onehot_t, _bf(ya_ref[pl.ds(r0, rows), :]))

        for_tiles(combine)

    @pl.when(jnp.logical_and(e == n_exp - 1, f == nf - 1))
    def _():
        ln = ln_ref[...]
        o_ref[...] = _layer_norm(alpha * x_ref[...] + acc_ref[...], ln[0:1, :], ln[1:2, :])


def moe_ln(x, gate, wg, wu, wd, j, ln, alpha):
    T, D = x.shape
    _, E, _, F = wg.shape
    tf = next(c for c in (896, 512, 256, 128) if F % c == 0)
    TB = MOE_TOKENS
    nb = T // TB
    selb = (gate[:, :E] > 0.0).reshape(nb, TB, E)
    rank = jnp.cumsum(selb.astype(jnp.int32), axis=1) - 1
    rank = jnp.where(selb, rank, -1).astype(F32)
    cnt = jnp.sum(selb.astype(jnp.int32), axis=1).reshape(-1)
    rk = jnp.pad(rank.reshape(T, E), ((0, 0), (0, LANE - E)), constant_values=-1.0)
    rkt = jnp.transpose(rank, (0, 2, 1))
    blk = lambda i, e, f, cnt: (i, 0)
    grid_spec = pltpu.PrefetchScalarGridSpec(
        num_scalar_prefetch=1,
        grid=(nb, E, F // tf),
        in_specs=[pl.BlockSpec((TB, D), blk),
                  pl.BlockSpec((1, E, TB), lambda i, e, f, cnt: (i, 0, 0)),
                  pl.BlockSpec((TB, LANE), blk), pl.BlockSpec((TB, LANE), blk),
                  pl.BlockSpec((1, 1, D, tf), lambda i, e, f, cnt: (j, e, 0, f)),
                  pl.BlockSpec((1, 1, D, tf), lambda i, e, f, cnt: (j, e, 0, f)),
                  pl.BlockSpec((1, 1, tf, D), lambda i, e, f, cnt: (j, e, f, 0)),
                  pl.BlockSpec((2, D), lambda i, e, f, cnt: (0, 0))],
        out_specs=pl.BlockSpec((TB, D), blk),
        scratch_shapes=[pltpu.VMEM((TB, D), BF16), pltpu.VMEM((TB, D), BF16),
                        pltpu.VMEM((TB, D), F32), pltpu.VMEM((TB, D), F32)],
    )
    return pl.pallas_call(
        functools.partial(_moe_kernel, alpha=alpha, n_exp=E),
        grid_spec=grid_spec,
        out_shape=jax.ShapeDtypeStruct((T, D), F32),
        compiler_params=_cparams(("parallel", "arbitrary", "arbitrary")),
        name="moe_ln",
    )(cnt, x, rkt, rk, gate, wg, wu, wd, ln)


def kernel(x_prompt, x_sample, cache_k, cache_v, state_delta, state_conv, page_table, w_in, conv_w,
           a_log, dt_bias, o_norm_w, sb_bias, w_proj_a, w_proj_b, w_out, ln1_g, ln1_b, ln2_g, ln2_b,
           ffn_w_gate, ffn_w_up, ffn_w_down, router_w, moe_w_gate, moe_w_up, moe_w_down):
    pb, seq, D = x_prompt.shape
    db, dseq, _ = x_sample.shape
    depth = w_in.shape[0]
    n_phys, page = cache_k.shape[1], cache_k.shape[2]
    n_pages = page_table.shape[1]
    W_b = H_B * D_HB
    C3 = 3 * H_A * DK_A
    alpha = (2.0 * depth) ** 0.25
    tp = pb * seq
    ts = db * SAMPLE_PAD

    xs_pad = jnp.pad(x_sample, ((0, 0), (0, SAMPLE_PAD - dseq), (0, 0)))
    x = jnp.concatenate([x_prompt.reshape(tp, D), xs_pad.reshape(ts, D)], axis=0)
    ck = jnp.transpose(cache_k, (0, 1, 3, 4, 2)).reshape(depth * n_phys, W_b, page)
    cv = jnp.transpose(cache_v, (0, 1, 3, 4, 2)).reshape(depth * n_phys, W_b, page)
    pt_flat = page_table.reshape(-1).astype(jnp.int32)
    buf0 = jnp.zeros((pb, CONV_W - 1, C3), F32)
    s00 = jnp.zeros((pb, H_A, DK_A, DV_A), F32)
    ffn_wg, ffn_wu, ffn_wd = _bf(ffn_w_gate), _bf(ffn_w_up), _bf(ffn_w_down)
    moe_wg, moe_wu, moe_wd = _bf(moe_w_gate), _bf(moe_w_up), _bf(moe_w_down)

    outs = {n: [] for n in ("sp", "cp", "ks", "vs", "ss", "cs")}
    kt_all = vt_all = None
    for li in range(depth):
        w_main, wkv_t = pack_w_in(w_in[li])
        qkv, z, gla, glb, bd, qb = in_projection_main(x, w_main)
        kt_all, vt_all, kt16, vt16 = in_projection_kv_t(x, wkv_t, pb, seq, li, depth, kt_all, vt_all)
        kb_s, vb_s = in_projection_kv_rows(x, wkv_t, tp, ts)
        head_par = jnp.zeros((2, LANE), F32)
        head_par = head_par.at[0, H_A:2 * H_A].set(-jnp.exp(a_log[li].astype(F32)))
        head_par = head_par.at[1, H_A:2 * H_A].set(dt_bias[li].astype(F32))
        norm_w = o_norm_w[li].reshape(1, DV_A).astype(F32)

        qp, kp_, vp_, cbuf_p = conv_prep(qkv, conv_w[li], buf0, 0, pb, seq, 512, seq)
        qs, ks_, vs_, cbuf_s = conv_prep(qkv, conv_w[li], state_conv[li], tp, db, SAMPLE_PAD,
                                         SAMPLE_PAD, dseq, G=16)
        oa_p, s_p = delta_rule_prompt(qp, kp_, vp_, bd, z, head_par, norm_w, s00, pb, seq,
                                      DELTA_CHUNK)
        oa_s, s_s = delta_rule_short(qs, ks_, vs_, bd, z, head_par, norm_w, state_delta[li],
                                     tp, db, SAMPLE_PAD, dseq)
        ob_p = sb_attention_prompt(qb, kt16, vt16, sb_bias[li].astype(F32), pb, seq)
        ob_s = sb_attention_sample(qb, kb_s, vb_s, sb_bias[li].astype(F32).reshape(H_B, 1), ck, cv,
                                   pt_flat, li * n_phys, tp, db, n_pages, dseq)
        oa = jnp.concatenate([oa_p, oa_s], axis=0)
        ob = jnp.concatenate([ob_p, _bf(ob_s)], axis=0)
        ln1 = jnp.stack([ln1_g[li], ln1_b[li]]).astype(F32)
        ln2 = jnp.stack([ln2_g[li], ln2_b[li]]).astype(F32)
        x1 = merge_out_ln(x, oa, ob, gla, glb, _bf(w_proj_a[li]), _bf(w_proj_b[li]), _bf(w_out[li]),
                          ln1, alpha)
        j = li // 2
        if li % 2 == 0:
            x = ffn_ln(x1, ffn_wg, ffn_wu, ffn_wd, j, ln2, alpha)
        else:
            gate = router_gate(x1, router_w[j].astype(F32))
            x = moe_ln(x1, gate, moe_wg, moe_wu, moe_wd, j, ln2, alpha)

        outs["sp"].append(s_p)
        outs["cp"].append(cbuf_p)
        outs["ks"].append(kb_s.reshape(db, SAMPLE_PAD, H_B, D_HB)[:, :dseq])
        outs["vs"].append(vb_s.reshape(db, SAMPLE_PAD, H_B, D_HB)[:, :dseq])
        outs["ss"].append(s_s)
        outs["cs"].append(cbuf_s)

    y_prompt = x[:tp].reshape(pb, seq, D)
    y_sample = x[tp:].reshape(db, SAMPLE_PAD, D)[:, :dseq]
    st = {n: jnp.stack(v) for n, v in outs.items()}
    st["kp"] = jnp.transpose(kt_all.reshape(depth, pb, H_B, D_HB, seq), (0, 1, 4, 2, 3))
    st["vp"] = jnp.transpose(vt_all.reshape(depth, pb, H_B, D_HB, seq), (0, 1, 4, 2, 3))
    return (y_prompt, y_sample, st["kp"], st["vp"], st["sp"], st["cp"],
            st["ks"], st["vs"], st["ss"], st["cs"])
```
